```python
import math
import jax, jax.numpy as jnp
from jax import lax
import numpy as np

D_MODEL = 2048
BATCH = 1
SEQ = 16384
DEPTH = 2
DEC_BATCH = 1
DEC_SEQ = 8192
PAST_LEN = 128

N_META = 16
EPS = 1e-6
HYENA_ORDER = 2
SHORT_CONV = 3
POS_EMB_DIM = 33
FILTER_HIDDEN = 64
DECAY_TARGET = 1e-2
FAST_DECAY_PCT = 0.3
SLOW_DECAY_PCT = 1.5
MLA_HEADS = 16
Q_LORA = 512
KV_LORA = 256
QK_NOPE = 128
QK_ROPE = 64
QK_HEAD = QK_NOPE + QK_ROPE
V_HEAD = 128
ROPE_BASE = 10000.0
Q_BLOCK = 128
D_FF = 7168
N_EXPERTS = 8
TOP_K = 2
D_FF_EXPERT = 7168

kernel_name = 'hyena_mla_moe_bidir_encoder'


def rms_norm(x, g):
    xf = x.astype(jnp.float32)
    y = xf * lax.rsqrt(jnp.mean(xf * xf, axis=-1, keepdims=True) + EPS)
    return (y * g.astype(jnp.float32)).astype(x.dtype)


def centred_short_conv(u, w, b):
    L = u.shape[1]
    pad = SHORT_CONV // 2
    up = jnp.pad(u, ((0, 0), (pad, SHORT_CONV - 1 - pad), (0, 0)))
    out = b.astype(u.dtype)
    for j in range(SHORT_CONV):
        out = out + up[:, j:j + L] * w[j].astype(u.dtype)
    return out


def hyena_filter_hidden(L, w1, b1, w2, b2, freq):
    f32 = jnp.float32
    t = jnp.linspace(0.0, 1.0, L, dtype=f32)[:, None]
    bands = (POS_EMB_DIM - 1) // 2
    w = (2.0 * math.pi / L) * jnp.arange(L, dtype=f32)[:, None]
    f = jnp.linspace(1e-4, bands - 1, bands, dtype=f32)[None, :]
    z = jnp.concatenate([t, jnp.cos(f * w), -jnp.sin(f * w)], axis=-1)
    hdn = jnp.sin(freq[0].astype(f32) * (z @ w1.astype(f32) + b1.astype(f32)))
    hdn = jnp.sin(freq[1].astype(f32) * (hdn @ w2.astype(f32) + b2.astype(f32)))
    return hdn, t


def hyena_filter_spectrum(hdn, t, w3_n, log_decay_n):
    f32 = jnp.float32
    window = jnp.exp(-t[:, :, None] * jnp.exp(log_decay_n.astype(f32))[None])
    h = jnp.einsum('lf,fcd->lcd', hdn, w3_n.astype(f32)) * window
    h_fwd, h_bwd = h[:, 0], h[:, 1]
    full = jnp.concatenate([h_fwd[:1] + h_bwd[:1], h_fwd[1:],
                            jnp.zeros_like(h_fwd[:1]), h_bwd[:0:-1]], axis=0)
    full = full * lax.rsqrt(jnp.sum(full * full, axis=0, keepdims=True) + EPS)
    return jnp.fft.rfft(full, axis=0)


def hyena_mixer(x, w_in, conv_w, conv_b, fw1, fb1, fw2, fb2, fw3, ffreq, flog_decay, skip, w_out):
    B, L, D = x.shape
    f32 = jnp.float32
    u = centred_short_conv(x @ w_in, conv_w, conv_b)
    v, g1, g2 = jnp.split(u, 3, axis=-1)
    hdn, t = hyena_filter_hidden(L, fw1, fb1, fw2, fb2, ffreq)
    z = v.astype(f32)
    for n, gate in enumerate((g1, g2)):
        k_spec = hyena_filter_spectrum(hdn, t, fw3[:, n], flog_decay[n])
        z_spec = jnp.fft.rfft(z, n=2 * L, axis=1)
        conv = jnp.fft.irfft(z_spec * k_spec[None], n=2 * L, axis=1)[:, :L]
        z = gate.astype(f32) * (conv + z * skip[n].astype(f32))
    return z.astype(x.dtype) @ w_out


def rope_tables(L):
    inv = 1.0 / (ROPE_BASE ** (jnp.arange(0, QK_ROPE, 2, dtype=jnp.float32) / QK_ROPE))
    ang = jnp.arange(L, dtype=jnp.float32)[:, None] * inv[None, :]
    return jnp.cos(ang), jnp.sin(ang)


def apply_rope(x, cos, sin):
    half = QK_ROPE // 2
    x1, x2 = x[..., :half], x[..., half:]
    c = cos[None, :, None, :].astype(x.dtype)
    s = sin[None, :, None, :].astype(x.dtype)
    return jnp.concatenate([x1 * c - x2 * s, x1 * s + x2 * c], axis=-1)


def attend_block(qb, k, v):
    s = jnp.einsum('bqhd,bkhd->bhqk', qb, k, preferred_element_type=jnp.float32) * (QK_HEAD ** -0.5)
    p = jax.nn.softmax(s, axis=-1)
    return jnp.einsum('bhqk,bkhd->bqhd', p.astype(v.dtype), v)


def bidirectional_attention(q, k, v):
    B, L, H, _ = q.shape
    S = L - N_META
    out_meta = attend_block(q[:, :N_META], k, v)
    qr = q[:, N_META:].reshape(B, S // Q_BLOCK, Q_BLOCK, H, QK_HEAD).swapaxes(0, 1)
    out_real = lax.map(lambda qb: attend_block(qb, k, v), qr)
    out_real = out_real.swapaxes(0, 1).reshape(B, S, H, V_HEAD)
    return jnp.concatenate([out_meta, out_real], axis=1)


def mla_mixer(x, w_dq, g_q, w_uq, w_dkv, g_kv, w_ukv, g_qhead, g_khead, w_o):
    B, L, _ = x.shape
    cq = rms_norm(x @ w_dq, g_q)
    q = (cq @ w_uq).reshape(B, L, MLA_HEADS, QK_HEAD)
    ckv_all = x @ w_dkv
    ckv, k_pe = ckv_all[..., :KV_LORA], ckv_all[..., KV_LORA:]
    kv = (rms_norm(ckv, g_kv) @ w_ukv).reshape(B, L, MLA_HEADS, QK_NOPE + V_HEAD)
    k_nope, v = kv[..., :QK_NOPE], kv[..., QK_NOPE:]
    k = jnp.concatenate([k_nope, jnp.broadcast_to(k_pe[:, :, None, :], (B, L, MLA_HEADS, QK_ROPE))], axis=-1)
    q = rms_norm(q, g_qhead)
    k = rms_norm(k, g_khead)
    cos, sin = rope_tables(L)
    q = jnp.concatenate([q[..., :QK_NOPE], apply_rope(q[..., QK_NOPE:], cos, sin)], axis=-1)
    k = jnp.concatenate([k[..., :QK_NOPE], apply_rope(k[..., QK_NOPE:], cos, sin)], axis=-1)
    o = bidirectional_attention(q, k, v)
    return o.reshape(B, L, MLA_HEADS * V_HEAD) @ w_o


def swiglu(x, w_gate, w_up, w_down):
    return (jax.nn.silu(x @ w_gate) * (x @ w_up)) @ w_down


def moe_swiglu(x, w_router, w_gate, w_up, w_down):
    B, L, D = x.shape
    xt = x.reshape(B * L, D)
    logits = jnp.dot(xt, w_router, preferred_element_type=jnp.float32)
    top_vals, top_idx = lax.top_k(logits, TOP_K)
    gates = jax.nn.softmax(top_vals, axis=-1)
    combine = jnp.sum(jax.nn.one_hot(top_idx, N_EXPERTS, dtype=jnp.float32) * gates[..., None], axis=1)
    combine = combine.astype(x.dtype)
    out = jnp.zeros_like(xt)
    for e in range(N_EXPERTS):
        out = out + combine[:, e:e + 1] * swiglu(xt, w_gate[e], w_up[e], w_down[e])
    return out.reshape(B, L, D)


def setup_inputs(seed: int = 0) -> dict:
    key = jax.random.key(seed)
    ks = iter(jax.random.split(key, 48))
    D = D_MODEL

    def nrm(shape, scale):
        return jax.random.normal(next(ks), shape, jnp.float32) * scale

    def gain(shape):
        return 1.0 + nrm(shape, 0.02)

    dec_lo = abs(math.log(DECAY_TARGET)) / SLOW_DECAY_PCT
    dec_hi = abs(math.log(DECAY_TARGET)) / FAST_DECAY_PCT
    log_decay_base = jnp.log(jnp.linspace(dec_lo, dec_hi, D, dtype=jnp.float32))
    return {
        'x_prompt': nrm((BATCH, SEQ, D), 1.0),
        'x_sample': nrm((DEC_BATCH, DEC_SEQ, D), 1.0),
        'meta_tokens': nrm((N_META, D), 1.0),
        'norm_mix': gain((DEPTH, D)),
        'norm_ffn': gain((DEPTH, D)),
        'hy_w_in': nrm((D, 3 * D), D ** -0.5),
        'hy_conv_w': nrm((SHORT_CONV, 3 * D), SHORT_CONV ** -0.5),
        'hy_conv_b': nrm((3 * D,), 0.02),
        'hy_filt_w1': nrm((POS_EMB_DIM, FILTER_HIDDEN), POS_EMB_DIM ** -0.5),
        'hy_filt_b1': nrm((FILTER_HIDDEN,), 0.1),
        'hy_filt_w2': nrm((FILTER_HIDDEN, FILTER_HIDDEN), FILTER_HIDDEN ** -0.5),
        'hy_filt_b2': nrm((FILTER_HIDDEN,), 0.1),
        'hy_filt_w3': nrm((FILTER_HIDDEN, HYENA_ORDER, 2, D), FILTER_HIDDEN ** -0.5),
        'hy_filt_freq': gain((2, FILTER_HIDDEN)),
        'hy_filt_log_decay': log_decay_base + nrm((HYENA_ORDER, 2, D), 0.05),
        'hy_skip': nrm((HYENA_ORDER, D), 0.5),
        'hy_w_out': nrm((D, D), D ** -0.5),
        'ffn_w_gate': nrm((D, D_FF), D ** -0.5),
        'ffn_w_up': nrm((D, D_FF), D ** -0.5),
        'ffn_w_down': nrm((D_FF, D), D_FF ** -0.5),
        'mla_w_dq': nrm((D, Q_LORA), D ** -0.5),
        'mla_g_q': gain((Q_LORA,)),
        'mla_w_uq': nrm((Q_LORA, MLA_HEADS * QK_HEAD), Q_LORA ** -0.5),
        'mla_w_dkv': nrm((D, KV_LORA + QK_ROPE), D ** -0.5),
        'mla_g_kv': gain((KV_LORA,)),
        'mla_w_ukv': nrm((KV_LORA, MLA_HEADS * (QK_NOPE + V_HEAD)), KV_LORA ** -0.5),
        'mla_g_qhead': gain((QK_HEAD,)),
        'mla_g_khead': gain((QK_HEAD,)),
        'mla_w_o': nrm((MLA_HEADS * V_HEAD, D), (MLA_HEADS * V_HEAD) ** -0.5),
        'moe_w_router': nrm((D, N_EXPERTS), D ** -0.5),
        'moe_w_gate': nrm((N_EXPERTS, D, D_FF_EXPERT), D ** -0.5),
        'moe_w_up': nrm((N_EXPERTS, D, D_FF_EXPERT), D ** -0.5),
        'moe_w_down': nrm((N_EXPERTS, D_FF_EXPERT, D), D_FF_EXPERT ** -0.5),
    }


def reference(x_prompt, x_sample, meta_tokens, norm_mix, norm_ffn,
              hy_w_in, hy_conv_w, hy_conv_b, hy_filt_w1, hy_filt_b1, hy_filt_w2, hy_filt_b2,
              hy_filt_w3, hy_filt_freq, hy_filt_log_decay, hy_skip, hy_w_out,
              ffn_w_gate, ffn_w_up, ffn_w_down,
              mla_w_dq, mla_g_q, mla_w_uq, mla_w_dkv, mla_g_kv, mla_w_ukv,
              mla_g_qhead, mla_g_khead, mla_w_o,
              moe_w_router, moe_w_gate, moe_w_up, moe_w_down):
    def trunk(x):
        B = x.shape[0]
        meta = jnp.broadcast_to(meta_tokens.astype(x.dtype)[None], (B, N_META, D_MODEL))
        h = jnp.concatenate([meta, x], axis=1)
        for i in range(DEPTH):
            hn = rms_norm(h, norm_mix[i])
            if i % 2 == 0:
                h = h + hyena_mixer(hn, hy_w_in, hy_conv_w, hy_conv_b, hy_filt_w1, hy_filt_b1,
                                    hy_filt_w2, hy_filt_b2, hy_filt_w3, hy_filt_freq,
                                    hy_filt_log_decay, hy_skip, hy_w_out)
            else:
                h = h + mla_mixer(hn, mla_w_dq, mla_g_q, mla_w_uq, mla_w_dkv, mla_g_kv,
                                  mla_w_ukv, mla_g_qhead, mla_g_khead, mla_w_o)
            hn = rms_norm(h, norm_ffn[i])
            if i % 2 == 0:
                h = h + swiglu(hn, ffn_w_gate, ffn_w_up, ffn_w_down)
            else:
                h = h + moe_swiglu(hn, moe_w_router, moe_w_gate, moe_w_up, moe_w_down)
        return h[:, N_META:]

    y_prompt = trunk(x_prompt)
    y_sample = trunk(x_sample)
    return (y_prompt, y_sample)
```

```python
import functools
import math

import numpy as np
import jax
import jax.numpy as jnp
from jax import lax
from jax.experimental import pallas as pl
from jax.experimental.pallas import tpu as pltpu

F32 = jnp.float32
BF16 = jnp.bfloat16

EPS = 1e-6
MLA_HEADS = 16
QK_NOPE = 128
QK_ROPE = 64
QK_HEAD = QK_NOPE + QK_ROPE
V_HEAD = 128
ROPE_BASE = 10000.0
N_EXPERTS = 8
POS_EMB_DIM = 33

LANE = 128
SUBLANE = 8
VMEM_LIMIT_BYTES = 56 * 1024 * 1024

FFT_C = 128
FFT_JB = SUBLANE

TOK_TILE = 512
NEG_BIG = -1e30


def _cparams(*sem):
    return pltpu.CompilerParams(dimension_semantics=sem, vmem_limit_bytes=VMEM_LIMIT_BYTES)


def _tile(n, pref):
    if n <= pref:
        return n
    t = pref - pref % LANE
    while n % t:
        t -= LANE
    return t


def _rms(x, g):
    ms = jnp.mean(x * x, axis=-1, keepdims=True)
    return x * lax.rsqrt(ms + EPS) * g


def _norm_mm_body(x_ref, g_ref, *rest, n_w, epilogue):
    w_refs, o_ref, xn_ref = rest[:n_w], rest[n_w], rest[n_w + 1]

    @pl.when(pl.program_id(1) == 0)
    def _():
        xn_ref[...] = _rms(x_ref[...], g_ref[...]).astype(BF16)

    xn = xn_ref[...]
    ys = [jnp.dot(xn, w[...], preferred_element_type=F32) for w in w_refs]
    o_ref[...] = epilogue(*ys).astype(o_ref.dtype)


def norm_mm(x, g, ws, epilogue, out_dtype, tn, tm=TOK_TILE):
    T, K = x.shape
    N = ws[0].shape[1]
    tn = _tile(N, tn)
    n_w = len(ws)
    return pl.pallas_call(
        functools.partial(_norm_mm_body, n_w=n_w, epilogue=epilogue),
        grid=(T // tm, N // tn),
        in_specs=[pl.BlockSpec((tm, K), lambda i, j: (i, 0)),
                  pl.BlockSpec((1, K), lambda i, j: (0, 0))]
        + [pl.BlockSpec((K, tn), lambda i, j: (0, j))] * n_w,
        out_specs=pl.BlockSpec((tm, tn), lambda i, j: (i, j)),
        out_shape=jax.ShapeDtypeStruct((T, N), out_dtype),
        scratch_shapes=[pltpu.VMEM((tm, K), BF16)],
        compiler_params=_cparams("parallel", "arbitrary"),
    )(x, g.reshape(1, K), *ws)


def _mm_res_body(x_ref, w_ref, r_ref, o_ref):
    o_ref[...] = r_ref[...] + jnp.dot(x_ref[...], w_ref[...], preferred_element_type=F32)


def mm_res(x, w, res, tn, tm=TOK_TILE):
    T, K = x.shape
    N = w.shape[1]
    tn = _tile(N, tn)
    return pl.pallas_call(
        _mm_res_body,
        grid=(T // tm, N // tn),
        in_specs=[pl.BlockSpec((tm, K), lambda i, j: (i, 0)),
                  pl.BlockSpec((K, tn), lambda i, j: (0, j)),
                  pl.BlockSpec((tm, tn), lambda i, j: (i, j))],
        out_specs=pl.BlockSpec((tm, tn), lambda i, j: (i, j)),
        out_shape=jax.ShapeDtypeStruct((T, N), F32),
        compiler_params=_cparams("parallel", "parallel"),
    )(x, w, res)


def _silu_mul(a, b):
    return a * (1.0 / (1.0 + jnp.exp(-a))) * b


def _short_conv_body(u_ref, prev_ref, next_ref, w_ref, b_ref, o_ref, *, row0, n_valid, n_rows_total, halo):
    i = pl.program_id(0)
    tm = u_ref.shape[0]
    u = u_ref[...].astype(F32)
    row = lax.broadcasted_iota(jnp.int32, u.shape, 0)
    g_row = row0 + i * tm
    prev_row = jnp.where(g_row == 0, 0.0, prev_ref[...].astype(F32)[halo - 1:halo, :])
    next_row = jnp.where(g_row + tm == n_rows_total, 0.0, next_ref[...].astype(F32)[0:1, :])
    up = jnp.where(row == 0, prev_row, pltpu.roll(u, 1, 0))
    dn = jnp.where(row == tm - 1, next_row, pltpu.roll(u, tm - 1, 0))
    y = b_ref[...] + up * w_ref[0:1, :] + u * w_ref[1:2, :] + dn * w_ref[2:3, :]
    o_ref[...] = jnp.where(i * tm + row < n_valid, y, 0.0)


def short_conv(u, w, b, row0, n_rows, n_valid, tm=TOK_TILE, tn=1024):
    T, C3 = u.shape
    tn = _tile(C3, tn)
    halo = 16
    r0 = row0 // tm
    hb = tm // halo
    nhalo = T // halo
    return pl.pallas_call(
        functools.partial(_short_conv_body, row0=row0, n_valid=n_valid, n_rows_total=T, halo=halo),
        grid=(n_rows // tm, C3 // tn),
        in_specs=[pl.BlockSpec((tm, tn), lambda i, j: (r0 + i, j)),
                  pl.BlockSpec((halo, tn), lambda i, j: (jnp.maximum((r0 + i) * hb - 1, 0), j)),
                  pl.BlockSpec((halo, tn), lambda i, j: (jnp.minimum((r0 + i + 1) * hb, nhalo - 1), j)),
                  pl.BlockSpec((3, tn), lambda i, j: (0, j)),
                  pl.BlockSpec((1, tn), lambda i, j: (0, j))],
        out_specs=pl.BlockSpec((tm, tn), lambda i, j: (i, j)),
        out_shape=jax.ShapeDtypeStruct((n_rows, C3), F32),
        compiler_params=_cparams("parallel", "parallel"),
    )(u, u, u, w, b.reshape(1, C3))


def _filter_body(z_ref, w1_ref, b1_ref, w2_ref, b2_ref, w3_ref, fr_ref, dec_ref, t_ref,
                 h_ref, ssq_ref, h0_ref, *, n_valid):
    i = pl.program_id(1)
    hi = lax.Precision.HIGHEST
    tm = z_ref.shape[0]
    a = jnp.dot(z_ref[...], w1_ref[...], precision=hi, preferred_element_type=F32) + b1_ref[...]
    a = jnp.sin(fr_ref[0:1, :] * a)
    a = jnp.dot(a, w2_ref[...], precision=hi, preferred_element_type=F32) + b2_ref[...]
    a = jnp.sin(fr_ref[1:2, :] * a)
    h = jnp.dot(a, w3_ref[...], precision=hi, preferred_element_type=F32)
    window = jnp.exp(-t_ref[...] * jnp.exp(dec_ref[...]))
    row = lax.broadcasted_iota(jnp.int32, h.shape, 0) + i * tm
    h = jnp.where(row < n_valid, h * window, 0.0)
    h_ref[...] = h.astype(h_ref.dtype)

    @pl.when(i == 0)
    def _():
        ssq_ref[...] = jnp.zeros_like(ssq_ref)
        h0_ref[...] = jnp.broadcast_to(h[0:1, :], h0_ref.shape)

    ssq_ref[...] += jnp.broadcast_to(jnp.sum(h * h, axis=0, keepdims=True), ssq_ref.shape)


def hyena_filter(zfeat, tcol, w1, b1, w2, b2, w3, freq, log_decay, n_valid, tm=512, tc=2048):
    P = zfeat.shape[0]
    FH, NC = w3.shape
    tc = _tile(NC, tc)
    return pl.pallas_call(
        functools.partial(_filter_body, n_valid=n_valid),
        grid=(NC // tc, P // tm),
        in_specs=[pl.BlockSpec((tm, LANE), lambda c, i: (i, 0)),
                  pl.BlockSpec((LANE, FH), lambda c, i: (0, 0)),
                  pl.BlockSpec((1, FH), lambda c, i: (0, 0)),
                  pl.BlockSpec((FH, FH), lambda c, i: (0, 0)),
                  pl.BlockSpec((1, FH), lambda c, i: (0, 0)),
                  pl.BlockSpec((FH, tc), lambda c, i: (0, c)),
                  pl.BlockSpec((2, FH), lambda c, i: (0, 0)),
                  pl.BlockSpec((1, tc), lambda c, i: (0, c)),
                  pl.BlockSpec((tm, 1), lambda c, i: (i, 0))],
        out_specs=[pl.BlockSpec((tm, tc), lambda c, i: (i, c)),
                   pl.BlockSpec((SUBLANE, tc), lambda c, i: (0, c)),
                   pl.BlockSpec((SUBLANE, tc), lambda c, i: (0, c))],
        out_shape=[jax.ShapeDtypeStruct((P, NC), BF16),
                   jax.ShapeDtypeStruct((SUBLANE, NC), F32),
                   jax.ShapeDtypeStruct((SUBLANE, NC), F32)],
        compiler_params=_cparams("parallel", "arbitrary"),
    )(zfeat, w1, b1.reshape(1, FH), w2, b2.reshape(1, FH), w3, freq, log_decay.reshape(1, NC), tcol)


def _fft_fwd_body(x_ref, g_ref, y_ref, *, n1):
    for j in range(FFT_JB):
        xj = x_ref[:, j, :].astype(BF16)
        y = jnp.dot(g_ref[j], xj, preferred_element_type=F32)
        y_ref[:, 0, j, :] = y[:n1].astype(y_ref.dtype)
        y_ref[:, 1, j, :] = y[n1:].astype(y_ref.dtype)


def fft_fwd(x3, col0, ncols, gfwd, td=512):
    R, C, _ = x3.shape
    n1 = gfwd.shape[1] // 2
    td = _tile(ncols, td)
    c0 = col0 // td
    return pl.pallas_call(
        functools.partial(_fft_fwd_body, n1=n1),
        grid=(C // FFT_JB, ncols // td),
        in_specs=[pl.BlockSpec((R, FFT_JB, td), lambda i, j: (0, i, c0 + j)),
                  pl.BlockSpec((FFT_JB, 2 * n1, R), lambda i, j: (i, 0, 0))],
        out_specs=pl.BlockSpec((n1, 2, FFT_JB, td), lambda i, j: (0, 0, i, j)),
        out_shape=jax.ShapeDtypeStruct((n1, 2, C, ncols), BF16),
        compiler_params=_cparams("parallel", "parallel"),
    )(x3, gfwd)


def _filter_spec_body(ya_ref, yb_ref, wf_ref, s_ref, k_ref):
    c2, td = ya_ref.shape[1] * ya_ref.shape[2], ya_ref.shape[3]
    c = c2 // 2
    sa = jnp.dot(wf_ref[...], ya_ref[0].reshape(c2, td), preferred_element_type=F32)
    sb = jnp.dot(wf_ref[...], yb_ref[0].reshape(c2, td), preferred_element_type=F32)
    s = s_ref[...]
    k_ref[0, 0] = ((sa[:c] + sb[:c]) * s).astype(k_ref.dtype)
    k_ref[0, 1] = ((sa[c:] - sb[c:]) * s).astype(k_ref.dtype)


def filter_spectrum(yh, order, D, wf, scale, td=512):
    n1, _, C, _ = yh.shape
    td = _tile(D, td)
    ca = (order * 2 * D) // td
    cb = (order * 2 * D + D) // td
    return pl.pallas_call(
        _filter_spec_body,
        grid=(n1, D // td),
        in_specs=[pl.BlockSpec((1, 2, C, td), lambda k, j: (k, 0, 0, ca + j)),
                  pl.BlockSpec((1, 2, C, td), lambda k, j: (k, 0, 0, cb + j)),
                  pl.BlockSpec((2 * C, 2 * C), lambda k, j: (0, 0)),
                  pl.BlockSpec((1, td), lambda k, j: (0, j))],
        out_specs=pl.BlockSpec((1, 2, C, td), lambda k, j: (k, 0, 0, j)),
        out_shape=jax.ShapeDtypeStruct((n1, 2, C, D), BF16),
        compiler_params=_cparams("parallel", "parallel"),
    )(yh, yh, wf, scale)


def _spectral_body(y_ref, k_ref, wf_ref, wi_ref, o_ref):
    c2, td = y_ref.shape[1] * y_ref.shape[2], y_ref.shape[3]
    c = c2 // 2
    x = jnp.dot(wf_ref[...], y_ref[0].reshape(c2, td), preferred_element_type=F32)
    xr, xi = x[:c], x[c:]
    kr, ki = k_ref[0, 0].astype(F32), k_ref[0, 1].astype(F32)
    z = jnp.concatenate([xr * kr - xi * ki, xr * ki + xi * kr], axis=0).astype(BF16)
    o = jnp.dot(wi_ref[...], z, preferred_element_type=F32)
    o_ref[0] = o.reshape(2, c, td).astype(o_ref.dtype)


def spectral_mul(y, k, wf, wi, td=512):
    n1, _, C, D = y.shape
    td = _tile(D, td)
    blk = pl.BlockSpec((1, 2, C, td), lambda a, j: (a, 0, 0, j))
    mat = pl.BlockSpec((2 * C, 2 * C), lambda a, j: (0, 0))
    return pl.pallas_call(
        _spectral_body,
        grid=(n1, D // td),
        in_specs=[blk, blk, mat, mat],
        out_specs=blk,
        out_shape=jax.ShapeDtypeStruct((n1, 2, C, D), BF16),
        compiler_params=_cparams("parallel", "parallel"),
    )(y, k, wf, wi)


def _fft_inv_body(y_ref, gi_ref, z_ref, gate_ref, skip_ref, o_ref, *, n1, n_valid):
    i = pl.program_id(0)
    R = z_ref.shape[0]
    n1_idx = lax.broadcasted_iota(jnp.int32, (R, z_ref.shape[2]), 0)
    for j in range(FFT_JB):
        conv = (jnp.dot(gi_ref[0, j], y_ref[:, 0, j, :], preferred_element_type=F32)
                + jnp.dot(gi_ref[1, j], y_ref[:, 1, j, :], preferred_element_type=F32))
        z = gate_ref[:, j, :] * (conv + z_ref[:, j, :] * skip_ref[...])
        seq = n1_idx * FFT_C + (i * FFT_JB + j)
        o_ref[:, j, :] = jnp.where(seq < n_valid, z, 0.0).astype(o_ref.dtype)


def fft_inv_gate(y, ginv, z3, zcol0, gate3, gcol0, skip, n_valid, out_dtype, td=512):
    n1, _, C, D = y.shape
    R = ginv.shape[2]
    td = _tile(D, td)
    zc, gc = zcol0 // td, gcol0 // td
    return pl.pallas_call(
        functools.partial(_fft_inv_body, n1=n1, n_valid=n_valid),
        grid=(C // FFT_JB, D // td),
        in_specs=[pl.BlockSpec((n1, 2, FFT_JB, td), lambda i, j: (0, 0, i, j)),
                  pl.BlockSpec((2, FFT_JB, R, n1), lambda i, j: (0, i, 0, 0)),
                  pl.BlockSpec((R, FFT_JB, td), lambda i, j: (0, i, zc + j)),
                  pl.BlockSpec((R, FFT_JB, td), lambda i, j: (0, i, gc + j)),
                  pl.BlockSpec((1, td), lambda i, j: (0, j))],
        out_specs=pl.BlockSpec((R, FFT_JB, td), lambda i, j: (0, i, j)),
        out_shape=jax.ShapeDtypeStruct((R, C, D), out_dtype),
        compiler_params=_cparams("parallel", "parallel"),
    )(y, ginv, z3, gate3, skip)


def _dft_tables(R):
    C = FFT_C
    n1c = 2 * R
    N = n1c * C
    k1 = jnp.arange(n1c, dtype=jnp.int32)[None, :, None]
    n1 = jnp.arange(R, dtype=jnp.int32)[None, None, :]
    n2 = jnp.arange(C, dtype=jnp.int32)[:, None, None]
    ang = (2.0 * math.pi / N) * ((k1 * (C * n1 + n2)) % N).astype(F32)
    gr, gim = jnp.cos(ang), -jnp.sin(ang)
    gfwd = jnp.concatenate([gr, gim], axis=1).astype(BF16)
    ginv = jnp.stack([gr.transpose(0, 2, 1), gim.transpose(0, 2, 1)]).astype(BF16)
    a = jnp.arange(C, dtype=jnp.int32)
    angc = (2.0 * math.pi / C) * ((a[:, None] * a[None, :]) % C).astype(F32)
    fr, fi = jnp.cos(angc), -jnp.sin(angc)
    wf = jnp.block([[fr, -fi], [fi, fr]]).astype(BF16)
    wi = jnp.block([[fr, fi], [-fi, fr]]).astype(BF16)
    return gfwd, ginv, wf, wi, N


def _pos_features(L, P):
    bands = (POS_EMB_DIM - 1) // 2
    i = jnp.arange(P, dtype=F32)
    t = jnp.linspace(0.0, 1.0, L, dtype=F32)
    t = jnp.concatenate([t, jnp.ones((P - L,), F32)])[:, None]
    w = (2.0 * math.pi / L) * i[:, None]
    f = jnp.linspace(1e-4, bands - 1, bands, dtype=F32)[None, :]
    z = jnp.concatenate([t, jnp.cos(f * w), -jnp.sin(f * w)], axis=-1)
    return jnp.pad(z, ((0, 0), (0, LANE - POS_EMB_DIM))), t


def hyena_group(uc, L, D, filt, skip):
    P = uc.shape[0]
    R = P // FFT_C
    w1p, b1, w2, b2, w3f, freq, dec = filt
    gfwd, ginv, wf, wi, N = _dft_tables(R)
    zfeat, tcol = _pos_features(L, P)
    hw, ssq, h0 = hyena_filter(zfeat, tcol, w1p, b1, w2, b2, w3f, freq, dec, L)
    n_orders = skip.shape[0]
    ssq = ssq[0].reshape(n_orders, 2, D)
    h0 = h0[0].reshape(n_orders, 2, D)
    total = ssq[:, 0] + ssq[:, 1] + 2.0 * h0[:, 0] * h0[:, 1]
    scale = lax.rsqrt(total + EPS) / N
    yh = fft_fwd(hw.reshape(R, FFT_C, hw.shape[1]), 0, hw.shape[1], gfwd)
    uc3 = uc.reshape(R, FFT_C, 3 * D)
    z3, zcol = uc3, 0
    for n in range(n_orders):
        kspec = filter_spectrum(yh, n, D, wf, scale[n:n + 1])
        y = fft_fwd(z3, zcol, D, gfwd)
        y = spectral_mul(y, kspec, wf, wi)
        last = n == n_orders - 1
        z3 = fft_inv_gate(y, ginv, z3, zcol, uc3, (n + 1) * D, skip[n:n + 1], L, BF16 if last else F32)
        zcol = 0
    return z3.reshape(P, D)


def _rope(x, rot_ref, cos_ref, sin_ref):
    xr = jnp.dot(x.astype(BF16), rot_ref[...], preferred_element_type=F32)
    return x * cos_ref[...] + xr * sin_ref[...]


def _q_head_body(c_ref, gq_ref, w_ref, gh_ref, rot_ref, cos_ref, sin_ref, q_ref, cn_ref, *, q_lora, scale):
    @pl.when(pl.program_id(1) == 0)
    def _():
        cn_ref[...] = _rms(c_ref[:, :q_lora], gq_ref[...]).astype(BF16)

    q = jnp.dot(cn_ref[...], w_ref[0], preferred_element_type=F32)
    q = _rope(_rms(q, gh_ref[...]), rot_ref, cos_ref, sin_ref)
    q_ref[0] = (q * scale).astype(q_ref.dtype)


def mla_q_heads(cqkv, g_q, w_uq_h, g_qhead, rot, cosf, sinf, q_lora, tm=TOK_TILE):
    T, CW = cqkv.shape
    H = w_uq_h.shape[0]
    return pl.pallas_call(
        functools.partial(_q_head_body, q_lora=q_lora, scale=QK_HEAD ** -0.5),
        grid=(T // tm, H),
        in_specs=[pl.BlockSpec((tm, CW), lambda i, h: (i, 0)),
                  pl.BlockSpec((1, q_lora), lambda i, h: (0, 0)),
                  pl.BlockSpec((1, q_lora, QK_HEAD), lambda i, h: (h, 0, 0)),
                  pl.BlockSpec((1, QK_HEAD), lambda i, h: (0, 0)),
                  pl.BlockSpec((QK_HEAD, QK_HEAD), lambda i, h: (0, 0)),
                  pl.BlockSpec((tm, QK_HEAD), lambda i, h: (i, 0)),
                  pl.BlockSpec((tm, QK_HEAD), lambda i, h: (i, 0))],
        out_specs=pl.BlockSpec((1, tm, QK_HEAD), lambda i, h: (h, i, 0)),
        out_shape=jax.ShapeDtypeStruct((H, T, QK_HEAD), BF16),
        scratch_shapes=[pltpu.VMEM((tm, q_lora), BF16)],
        compiler_params=_cparams("parallel", "arbitrary"),
    )(cqkv, g_q.reshape(1, -1), w_uq_h, g_qhead.reshape(1, -1), rot, cosf, sinf)


def _kv_head_body(c_ref, gkv_ref, w_ref, gh_ref, rot_ref, cos_ref, sin_ref, k_ref, v_ref, cn_ref,
                  *, q_lora, kv_lora):
    @pl.when(pl.program_id(1) == 0)
    def _():
        cn_ref[...] = _rms(c_ref[:, q_lora:q_lora + kv_lora], gkv_ref[...]).astype(BF16)

    kv = jnp.dot(cn_ref[...], w_ref[0], preferred_element_type=F32)
    k_pe = c_ref[:, q_lora + kv_lora:q_lora + kv_lora + QK_ROPE]
    k = jnp.concatenate([kv[:, :QK_NOPE], k_pe], axis=-1)
    k = _rope(_rms(k, gh_ref[...]), rot_ref, cos_ref, sin_ref)
    k_ref[0] = k.astype(k_ref.dtype)
    v_ref[0] = kv[:, QK_NOPE:].astype(v_ref.dtype)


def mla_kv_heads(cqkv, g_kv, w_ukv_h, g_khead, rot, cosf, sinf, q_lora, kv_lora, tm=TOK_TILE):
    T, CW = cqkv.shape
    H = w_ukv_h.shape[0]
    return pl.pallas_call(
        functools.partial(_kv_head_body, q_lora=q_lora, kv_lora=kv_lora),
        grid=(T // tm, H),
        in_specs=[pl.BlockSpec((tm, CW), lambda i, h: (i, 0)),
                  pl.BlockSpec((1, kv_lora), lambda i, h: (0, 0)),
                  pl.BlockSpec((1, kv_lora, QK_NOPE + V_HEAD), lambda i, h: (h, 0, 0)),
                  pl.BlockSpec((1, QK_HEAD), lambda i, h: (0, 0)),
                  pl.BlockSpec((QK_HEAD, QK_HEAD), lambda i, h: (0, 0)),
                  pl.BlockSpec((tm, QK_HEAD), lambda i, h: (i, 0)),
                  pl.BlockSpec((tm, QK_HEAD), lambda i, h: (i, 0))],
        out_specs=[pl.BlockSpec((1, tm, QK_HEAD), lambda i, h: (h, i, 0)),
                   pl.BlockSpec((1, tm, V_HEAD), lambda i, h: (h, i, 0))],
        out_shape=[jax.ShapeDtypeStruct((H, T, QK_HEAD), BF16),
                   jax.ShapeDtypeStruct((H, T, V_HEAD), BF16)],
        scratch_shapes=[pltpu.VMEM((tm, kv_lora), BF16)],
        compiler_params=_cparams("parallel", "arbitrary"),
    )(cqkv, g_kv.reshape(1, -1), w_ukv_h, g_khead.reshape(1, -1), rot, cosf, sinf)


def _flash_body(q_ref, k_ref, v_ref, o_prev_ref, o_ref, m_ref, l_ref, acc_ref, *, n_valid):
    del o_prev_ref
    ik = pl.program_id(2)
    tk = k_ref.shape[1]

    @pl.when(ik == 0)
    def _():
        m_ref[...] = jnp.full_like(m_ref, NEG_BIG)
        l_ref[...] = jnp.zeros_like(l_ref)
        acc_ref[...] = jnp.zeros_like(acc_ref)

    s = lax.dot_general(q_ref[0], k_ref[0], (((1,), (1,)), ((), ())), preferred_element_type=F32)
    col = lax.broadcasted_iota(jnp.int32, s.shape, 1) + ik * tk
    s = jnp.where(col < n_valid, s, NEG_BIG)
    m_old = m_ref[...]
    m_new = jnp.maximum(m_old, jnp.max(s, axis=-1, keepdims=True))
    alpha = jnp.exp(m_old - m_new)
    p = jnp.exp(s - m_new)
    l_ref[...] = alpha * l_ref[...] + jnp.sum(p, axis=-1, keepdims=True)
    acc_ref[...] = alpha * acc_ref[...] + jnp.dot(p.astype(BF16), v_ref[0], preferred_element_type=F32)
    m_ref[...] = m_new

    @pl.when(ik == pl.num_programs(2) - 1)
    def _():
        o_ref[...] = (acc_ref[...] / l_ref[...]).astype(o_ref.dtype)


def flash_attention(q, k, v, o_prev, row0, n_rows, n_valid, tq=512, tk=512):
    H, T, _ = q.shape
    q0, k0 = row0 // tq, row0 // tk
    nk = pl.cdiv(n_valid, tk)
    return pl.pallas_call(
        functools.partial(_flash_body, n_valid=n_valid),
        grid=(H, n_rows // tq, nk),
        in_specs=[pl.BlockSpec((1, tq, QK_HEAD), lambda h, i, j: (h, q0 + i, 0)),
                  pl.BlockSpec((1, tk, QK_HEAD), lambda h, i, j: (h, k0 + j, 0)),
                  pl.BlockSpec((1, tk, V_HEAD), lambda h, i, j: (h, k0 + j, 0)),
                  pl.BlockSpec(memory_space=pl.ANY)],
        out_specs=pl.BlockSpec((tq, V_HEAD), lambda h, i, j: (q0 + i, h)),
        out_shape=jax.ShapeDtypeStruct((T, H * V_HEAD), BF16),
        scratch_shapes=[pltpu.VMEM((tq, 1), F32), pltpu.VMEM((tq, 1), F32), pltpu.VMEM((tq, V_HEAD), F32)],
        input_output_aliases={3: 0},
        compiler_params=_cparams("parallel", "parallel", "arbitrary"),
    )(q, k, v, o_prev)


META_E1, META_E2, META_G1, META_G2, META_R1, META_R2 = range(6)


def _router_body(h_ref, g_ref, wr_ref, xn_ref, meta_ref, cnt_ref, carry_ref):
    i = pl.program_id(0)

    @pl.when(i == 0)
    def _():
        carry_ref[...] = jnp.zeros_like(carry_ref)

    xn = _rms(h_ref[...], g_ref[...])
    xn_ref[...] = xn
    logits = jnp.dot(xn, wr_ref[...], precision=lax.Precision.HIGHEST, preferred_element_type=F32)
    tb = logits.shape[0]
    lane = lax.broadcasted_iota(jnp.int32, logits.shape, 1)
    lg = jnp.where(lane < N_EXPERTS, logits, NEG_BIG)
    m1 = jnp.max(lg, axis=-1, keepdims=True)
    i1 = jnp.min(jnp.where(lg == m1, lane, LANE), axis=-1, keepdims=True)
    lg2 = jnp.where(lane == i1, NEG_BIG, lg)
    m2 = jnp.max(lg2, axis=-1, keepdims=True)
    i2 = jnp.min(jnp.where(lg2 == m2, lane, LANE), axis=-1, keepdims=True)
    e = jnp.exp(m2 - m1)
    g1 = 1.0 / (1.0 + e)
    g2 = e / (1.0 + e)
    oh = jnp.where((lane == i1) | (lane == i2), 1.0, 0.0)
    r = lax.broadcasted_iota(jnp.int32, (tb, tb), 0)
    c = lax.broadcasted_iota(jnp.int32, (tb, tb), 1)
    tri = jnp.where(c < r, 1.0, 0.0).astype(BF16)
    pre = jnp.dot(tri, oh.astype(BF16), preferred_element_type=F32) + carry_ref[0:1, :]
    r1 = jnp.sum(jnp.where(lane == i1, pre, 0.0), axis=-1, keepdims=True)
    r2 = jnp.sum(jnp.where(lane == i2, pre, 0.0), axis=-1, keepdims=True)
    carry_ref[...] += jnp.broadcast_to(jnp.sum(oh, axis=0, keepdims=True), carry_ref.shape)
    meta = jnp.zeros(logits.shape, F32)
    for idx, val in ((META_E1, i1.astype(F32)), (META_E2, i2.astype(F32)), (META_G1, g1), (META_G2, g2),
                     (META_R1, r1), (META_R2, r2)):
        meta = jnp.where(lane == idx, val, meta)
    meta_ref[...] = meta
    cnt_ref[...] = carry_ref[...]


def moe_router(h, g, w_router_pad, tb=256):
    T, D = h.shape
    return pl.pallas_call(
        _router_body,
        grid=(T // tb,),
        in_specs=[pl.BlockSpec((tb, D), lambda i: (i, 0)),
                  pl.BlockSpec((1, D), lambda i: (0, 0)),
                  pl.BlockSpec((D, LANE), lambda i: (0, 0))],
        out_specs=[pl.BlockSpec((tb, D), lambda i: (i, 0)),
                   pl.BlockSpec((tb, LANE), lambda i: (i, 0)),
                   pl.BlockSpec((SUBLANE, LANE), lambda i: (0, 0))],
        out_shape=[jax.ShapeDtypeStruct((T, D), F32),
                   jax.ShapeDtypeStruct((T, LANE), F32),
                   jax.ShapeDtypeStruct((SUBLANE, LANE), F32)],
        scratch_shapes=[pltpu.VMEM((SUBLANE, LANE), F32)],
        compiler_params=_cparams("arbitrary"),
    )(h, g.reshape(1, D), w_router_pad)


def _dispatch_body(pos_ref, xn_hbm, xs_in, xs_hbm, sem, *, tb):
    del xs_in
    i = pl.program_id(0)

    def row_copy(r, k):
        return pltpu.make_async_copy(xn_hbm.at[pl.ds(i * tb + r, 1)],
                                     xs_hbm.at[pl.ds(pos_ref[0, 0, 2 * r + k], 1)], sem)

    def start(r, c):
        row_copy(r, 0).start()
        row_copy(r, 1).start()
        return c

    def wait(r, c):
        row_copy(r, 0).wait()
        row_copy(r, 1).wait()
        return c

    lax.fori_loop(0, tb, start, 0)
    lax.fori_loop(0, tb, wait, 0)


def moe_dispatch(xn, pos, xs_init, tb=256):
    T, D = xn.shape
    pos3 = pos.reshape(T // tb, 1, 2 * tb)
    return pl.pallas_call(
        functools.partial(_dispatch_body, tb=tb),
        grid=(T // tb,),
        in_specs=[pl.BlockSpec((1, 1, 2 * tb), lambda i: (i, 0, 0), memory_space=pltpu.SMEM),
                  pl.BlockSpec(memory_space=pl.ANY),
                  pl.BlockSpec(memory_space=pl.ANY)],
        out_specs=pl.BlockSpec(memory_space=pl.ANY),
        out_shape=jax.ShapeDtypeStruct(xs_init.shape, xs_init.dtype),
        scratch_shapes=[pltpu.SemaphoreType.DMA(())],
        input_output_aliases={2: 0},
        compiler_params=_cparams("arbitrary"),
    )(pos3, xn, xs_init)


def _expert_up_body(te_ref, nu_ref, x_ref, wg_ref, wu_ref, o_ref):
    @pl.when(pl.program_id(0) < nu_ref[0])
    def _():
        x = x_ref[...].astype(BF16)
        a = jnp.dot(x, wg_ref[0], preferred_element_type=F32)
        b = jnp.dot(x, wu_ref[0], preferred_element_type=F32)
        o_ref[...] = _silu_mul(a, b).astype(o_ref.dtype)

    @pl.when(pl.program_id(0) >= nu_ref[0])
    def _():
        o_ref[...] = jnp.zeros_like(o_ref)


def expert_up(xs, wg, wu, tile_expert, n_used, tm, tf=1024):
    S, D = xs.shape
    F = wg.shape[2]
    tf = _tile(F, tf)
    grid_spec = pltpu.PrefetchScalarGridSpec(
        num_scalar_prefetch=2,
        grid=(S // tm, F // tf),
        in_specs=[pl.BlockSpec((tm, D), lambda i, j, te, nu: (i, 0)),
                  pl.BlockSpec((1, D, tf), lambda i, j, te, nu: (te[i], 0, j)),
                  pl.BlockSpec((1, D, tf), lambda i, j, te, nu: (te[i], 0, j))],
        out_specs=pl.BlockSpec((tm, tf), lambda i, j, te, nu: (i, j)),
    )
    return pl.pallas_call(
        _expert_up_body,
        grid_spec=grid_spec,
        out_shape=jax.ShapeDtypeStruct((S, F), BF16),
        compiler_params=_cparams("parallel", "parallel"),
    )(tile_expert, n_used, xs, wg, wu)


def _expert_down_body(te_ref, nu_ref, a_ref, w_ref, o_ref):
    @pl.when(pl.program_id(0) < nu_ref[0])
    def _():
        o_ref[...] = jnp.dot(a_ref[...], w_ref[0], preferred_element_type=F32)

    @pl.when(pl.program_id(0) >= nu_ref[0])
    def _():
        o_ref[...] = jnp.zeros_like(o_ref)


def expert_down(act, wd, tile_expert, n_used, tm, tn=512):
    S, F = act.shape
    D = wd.shape[2]
    tn = _tile(D, tn)
    grid_spec = pltpu.PrefetchScalarGridSpec(
        num_scalar_prefetch=2,
        grid=(S // tm, D // tn),
        in_specs=[pl.BlockSpec((tm, F), lambda i, j, te, nu: (i, 0)),
                  pl.BlockSpec((1, F, tn), lambda i, j, te, nu: (te[i], 0, j))],
        out_specs=pl.BlockSpec((tm, tn), lambda i, j, te, nu: (i, j)),
    )
    return pl.pallas_call(
        _expert_down_body,
        grid_spec=grid_spec,
        out_shape=jax.ShapeDtypeStruct((S, D), F32),
        compiler_params=_cparams("parallel", "parallel"),
    )(tile_expert, n_used, act, wd)


def _combine_body(pos_ref, h_ref, meta_ref, ys_hbm, o_ref, buf, sem, *, tb):
    def row_copy(r, k):
        return pltpu.make_async_copy(ys_hbm.at[pl.ds(pos_ref[0, 0, 2 * r + k], 1)],
                                     buf.at[k, pl.ds(r, 1)], sem)

    def start(r, c):
        row_copy(r, 0).start()
        row_copy(r, 1).start()
        return c

    def wait(r, c):
        row_copy(r, 0).wait()
        row_copy(r, 1).wait()
        return c

    lax.fori_loop(0, tb, start, 0)
    lax.fori_loop(0, tb, wait, 0)
    g1 = meta_ref[:, META_G1:META_G1 + 1]
    g2 = meta_ref[:, META_G2:META_G2 + 1]
    o_ref[...] = h_ref[...] + g1 * buf[0] + g2 * buf[1]


def moe_combine(h, meta, pos, ys, tb=256):
    T, D = h.shape
    pos3 = pos.reshape(T // tb, 1, 2 * tb)
    return pl.pallas_call(
        functools.partial(_combine_body, tb=tb),
        grid=(T // tb,),
        in_specs=[pl.BlockSpec((1, 1, 2 * tb), lambda i: (i, 0, 0), memory_space=pltpu.SMEM),
                  pl.BlockSpec((tb, D), lambda i: (i, 0)),
                  pl.BlockSpec((tb, LANE), lambda i: (i, 0)),
                  pl.BlockSpec(memory_space=pl.ANY)],
        out_specs=pl.BlockSpec((tb, D), lambda i: (i, 0)),
        out_shape=jax.ShapeDtypeStruct((T, D), F32),
        scratch_shapes=[pltpu.VMEM((2, tb, D), F32), pltpu.SemaphoreType.DMA(())],
        compiler_params=_cparams("arbitrary"),
    )(pos3, h, meta, ys)


def moe_layer(h, g, w_router, wg, wu, wd, tm=512):
    T, D = h.shape
    wr = jnp.pad(w_router, ((0, 0), (0, LANE - N_EXPERTS)))
    xn, meta, cnt = moe_router(h, g, wr)
    counts = cnt[0, :N_EXPERTS].astype(jnp.int32)
    tiles = (counts + tm - 1) // tm
    tile_end = jnp.cumsum(tiles)
    starts = (tile_end - tiles) * tm
    n_tiles = (2 * T + N_EXPERTS * (tm - 1)) // tm
    e_idx = meta[:, META_E1:META_E2 + 1].astype(jnp.int32)
    rank = meta[:, META_R1:META_R2 + 1].astype(jnp.int32)
    pos = starts[e_idx] + rank
    tile_expert = jnp.minimum(jnp.sum(jnp.arange(n_tiles)[:, None] >= tile_end[None, :], axis=1),
                              N_EXPERTS - 1).astype(jnp.int32)
    n_used = tile_end[-1:].astype(jnp.int32)
    xs = moe_dispatch(xn, pos, jnp.zeros((n_tiles * tm, D), F32))
    act = expert_up(xs, wg, wu, tile_expert, n_used, tm)
    ys = expert_down(act, wd, tile_expert, n_used, tm)
    return moe_combine(h, meta, pos, ys)


def _padded_len(L):
    r = -(-L // FFT_C)
    r = -(-r // SUBLANE) * SUBLANE
    return r * FFT_C


def _rope_tables(lens, pads):
    half = QK_ROPE // 2
    inv = 1.0 / (ROPE_BASE ** (jnp.arange(0, QK_ROPE, 2, dtype=F32) / QK_ROPE))
    cs, ss = [], []
    for L, P in zip(lens, pads):
        ang = jnp.arange(P, dtype=F32)[:, None] * inv[None, :]
        c, s = jnp.cos(ang), jnp.sin(ang)
        cs.append(jnp.concatenate([jnp.ones((P, QK_NOPE), F32), c, c], axis=-1))
        ss.append(jnp.concatenate([jnp.zeros((P, QK_NOPE), F32), s, s], axis=-1))
    rot = np.zeros((QK_HEAD, QK_HEAD), np.float32)
    for c in range(QK_NOPE, QK_NOPE + half):
        rot[c + half, c] = -1.0
        rot[c, c + half] = 1.0
    return jnp.concatenate(cs), jnp.concatenate(ss), jnp.asarray(rot, BF16)


def kernel(x_prompt, x_sample, meta_tokens, norm_mix, norm_ffn, hy_w_in, hy_conv_w, hy_conv_b, hy_filt_w1, hy_filt_b1, hy_filt_w2, hy_filt_b2, hy_filt_w3, hy_filt_freq, hy_filt_log_decay, hy_skip, hy_w_out, ffn_w_gate, ffn_w_up, ffn_w_down, mla_w_dq, mla_g_q, mla_w_uq, mla_w_dkv, mla_g_kv, mla_w_ukv, mla_g_qhead, mla_g_khead, mla_w_o, moe_w_router, moe_w_gate, moe_w_up, moe_w_down):
    D = x_prompt.shape[-1]
    n_meta = meta_tokens.shape[0]
    assert x_prompt.shape[0] == 1 and x_sample.shape[0] == 1
    groups = (x_prompt[0], x_sample[0])
    lens = [n_meta + g.shape[0] for g in groups]
    pads = [_padded_len(L) for L in lens]
    row0s = [0, pads[0]]
    T = sum(pads)
    assert T % TOK_TILE == 0 and all(p % TOK_TILE == 0 for p in pads)

    parts = []
    for g, L, P in zip(groups, lens, pads):
        parts += [meta_tokens.astype(F32), g, jnp.zeros((P - L, D), F32)]
    h = jnp.concatenate(parts, axis=0)

    bf = lambda w: w.astype(BF16)

    u = norm_mm(h, norm_mix[0], [bf(hy_w_in)], lambda y: y, BF16, tn=1024)
    n_orders = hy_skip.shape[0]
    fh = hy_filt_w2.shape[0]
    filt = (jnp.pad(hy_filt_w1, ((0, LANE - POS_EMB_DIM), (0, 0))), hy_filt_b1, hy_filt_w2, hy_filt_b2,
            hy_filt_w3.reshape(fh, n_orders * 2 * D), hy_filt_freq, hy_filt_log_decay.reshape(-1))
    zs = []
    for L, P, r0 in zip(lens, pads, row0s):
        uc = short_conv(u, hy_conv_w, hy_conv_b, r0, P, L)
        zs.append(hyena_group(uc, L, D, filt, hy_skip))
    z = jnp.concatenate(zs, axis=0)
    h = mm_res(z, bf(hy_w_out), h, tn=1024)

    a = norm_mm(h, norm_ffn[0], [bf(ffn_w_gate), bf(ffn_w_up)], _silu_mul, BF16, tn=1024)
    h = mm_res(a, bf(ffn_w_down), h, tn=512)

    q_lora = mla_w_dq.shape[1]
    kv_lora = mla_w_ukv.shape[0]
    cw = q_lora + mla_w_dkv.shape[1]
    cw_pad = -(-cw // LANE) * LANE
    w_down = jnp.pad(jnp.concatenate([mla_w_dq, mla_w_dkv], axis=1), ((0, 0), (0, cw_pad - cw)))
    cqkv = norm_mm(h, norm_mix[1], [bf(w_down)], lambda y: y, F32, tn=cw_pad)
    cosf, sinf, rot = _rope_tables(lens, pads)
    w_uq_h = bf(mla_w_uq.reshape(q_lora, MLA_HEADS, QK_HEAD).transpose(1, 0, 2))
    w_ukv_h = bf(mla_w_ukv.reshape(kv_lora, MLA_HEADS, QK_NOPE + V_HEAD).transpose(1, 0, 2))
    q = mla_q_heads(cqkv, mla_g_q, w_uq_h, mla_g_qhead, rot, cosf, sinf, q_lora)
    k, v = mla_kv_heads(cqkv, mla_g_kv, w_ukv_h, mla_g_khead, rot, cosf, sinf, q_lora, kv_lora)
    o = jnp.zeros((T, MLA_HEADS * V_HEAD), BF16)
    for L, P, r0 in zip(lens, pads, row0s):
        o = flash_attention(q, k, v, o, r0, P, L)
    h = mm_res(o, bf(mla_w_o), h, tn=1024)

    h = moe_layer(h, norm_ffn[1], moe_w_router, bf(moe_w_gate), bf(moe_w_up), bf(moe_w_down))

    outs = []
    for g, r0 in zip(groups, row0s):
        outs.append(h[r0 + n_meta:r0 + n_meta + g.shape[0]][None])
    return tuple(outs)
```

```python
import functools
import math

import numpy as np
import jax
import jax.numpy as jnp
from jax import lax
from jax.experimental import pallas as pl
from jax.experimental.pallas import tpu as pltpu

F32 = jnp.float32
BF16 = jnp.bfloat16

EPS = 1e-6
MLA_HEADS = 16
QK_NOPE = 128
QK_ROPE = 64
QK_HEAD = QK_NOPE + QK_ROPE
V_HEAD = 128
VT_ROWS = V_HEAD + 16
ROPE_BASE = 10000.0
N_EXPERTS = 8
POS_EMB_DIM = 33

LANE = 128
SUBLANE = 8
VMEM_LIMIT_BYTES = 56 * 1024 * 1024

FFT_C = 128
FFT_JB = SUBLANE
FFT_KB = SUBLANE

TOK_TILE = 512
NEG_BIG = -1e30


def _cparams(*sem):
    return pltpu.CompilerParams(dimension_semantics=sem, vmem_limit_bytes=VMEM_LIMIT_BYTES)


def _tile(n, pref):
    if n <= pref:
        return n
    t = pref - pref % LANE
    while n % t:
        t -= LANE
    return t


def _rms(x, g):
    ms = jnp.mean(x * x, axis=-1, keepdims=True)
    return x * lax.rsqrt(ms + EPS) * g


def _norm_mm_body(x_ref, g_ref, *rest, n_w, epilogue):
    w_refs, o_ref, xn_ref = rest[:n_w], rest[n_w], rest[n_w + 1]

    @pl.when(pl.program_id(1) == 0)
    def _():
        xn_ref[...] = _rms(x_ref[...], g_ref[...]).astype(BF16)

    xn = xn_ref[...]
    ys = [jnp.dot(xn, w[...], preferred_element_type=F32) for w in w_refs]
    o_ref[...] = epilogue(*ys).astype(o_ref.dtype)


def norm_mm(x, g, ws, epilogue, out_dtype, tn, name, tm=TOK_TILE):
    T, K = x.shape
    N = ws[0].shape[1]
    tn = _tile(N, tn)
    n_w = len(ws)
    return pl.pallas_call(
        functools.partial(_norm_mm_body, n_w=n_w, epilogue=epilogue),
        grid=(T // tm, N // tn),
        in_specs=[pl.BlockSpec((tm, K), lambda i, j: (i, 0)),
                  pl.BlockSpec((1, K), lambda i, j: (0, 0))]
        + [pl.BlockSpec((K, tn), lambda i, j: (0, j))] * n_w,
        out_specs=pl.BlockSpec((tm, tn), lambda i, j: (i, j)),
        out_shape=jax.ShapeDtypeStruct((T, N), out_dtype),
        scratch_shapes=[pltpu.VMEM((tm, K), BF16)],
        compiler_params=_cparams("parallel", "arbitrary"),
        name=name,
    )(x, g.reshape(1, K), *ws)


def _mm_res_body(x_ref, w_ref, r_ref, o_ref):
    o_ref[...] = r_ref[...] + jnp.dot(x_ref[...].astype(BF16), w_ref[...], preferred_element_type=F32)


def mm_res(x, w, res, tn, name, tm=TOK_TILE):
    T, K = x.shape
    N = w.shape[1]
    tn = _tile(N, tn)
    return pl.pallas_call(
        _mm_res_body,
        grid=(T // tm, N // tn),
        in_specs=[pl.BlockSpec((tm, K), lambda i, j: (i, 0)),
                  pl.BlockSpec((K, tn), lambda i, j: (0, j)),
                  pl.BlockSpec((tm, tn), lambda i, j: (i, j))],
        out_specs=pl.BlockSpec((tm, tn), lambda i, j: (i, j)),
        out_shape=jax.ShapeDtypeStruct((T, N), F32),
        compiler_params=_cparams("parallel", "parallel"),
        name=name,
    )(x, w, res)


def _silu_mul(a, b):
    return a * (1.0 / (1.0 + jnp.exp(-a))) * b


def _short_conv_body(u_ref, prev_ref, next_ref, w_ref, b_ref, o_ref, *, row0, n_valid, n_rows_total, halo):
    i = pl.program_id(0)
    tm = u_ref.shape[0]
    u = u_ref[...].astype(F32)
    row = lax.broadcasted_iota(jnp.int32, u.shape, 0)
    g_row = row0 + i * tm
    prev_row = jnp.where(g_row == 0, 0.0, prev_ref[...].astype(F32)[halo - 1:halo, :])
    next_row = jnp.where(g_row + tm == n_rows_total, 0.0, next_ref[...].astype(F32)[0:1, :])
    up = jnp.where(row == 0, prev_row, pltpu.roll(u, 1, 0))
    dn = jnp.where(row == tm - 1, next_row, pltpu.roll(u, tm - 1, 0))
    y = b_ref[...] + up * w_ref[0:1, :] + u * w_ref[1:2, :] + dn * w_ref[2:3, :]
    o_ref[...] = jnp.where(i * tm + row < n_valid, y, 0.0)


def short_conv(u, w, b, row0, n_rows, n_valid, tm=TOK_TILE, tn=1024):
    T, C3 = u.shape
    tn = _tile(C3, tn)
    halo = 16
    r0 = row0 // tm
    hb = tm // halo
    nhalo = T // halo
    return pl.pallas_call(
        functools.partial(_short_conv_body, row0=row0, n_valid=n_valid, n_rows_total=T, halo=halo),
        grid=(n_rows // tm, C3 // tn),
        in_specs=[pl.BlockSpec((tm, tn), lambda i, j: (r0 + i, j)),
                  pl.BlockSpec((halo, tn), lambda i, j: (jnp.maximum((r0 + i) * hb - 1, 0), j)),
                  pl.BlockSpec((halo, tn), lambda i, j: (jnp.minimum((r0 + i + 1) * hb, nhalo - 1), j)),
                  pl.BlockSpec((3, tn), lambda i, j: (0, j)),
                  pl.BlockSpec((1, tn), lambda i, j: (0, j))],
        out_specs=pl.BlockSpec((tm, tn), lambda i, j: (i, j)),
        out_shape=jax.ShapeDtypeStruct((n_rows, C3), F32),
        compiler_params=_cparams("parallel", "parallel"),
        name="hyena_short_conv",
    )(u, u, u, w, b.reshape(1, C3))


def _filter_body(z_ref, w1_ref, b1_ref, w2_ref, b2_ref, w3_ref, fr_ref, dec_ref, t_ref,
                 h_ref, ssq_ref, h0_ref, *, n_valid):
    i = pl.program_id(1)
    hi = lax.Precision.HIGHEST
    tm = z_ref.shape[0]
    a = jnp.dot(z_ref[...], w1_ref[...], precision=hi, preferred_element_type=F32) + b1_ref[...]
    a = jnp.sin(fr_ref[0:1, :] * a)
    a = jnp.dot(a, w2_ref[...], precision=hi, preferred_element_type=F32) + b2_ref[...]
    a = jnp.sin(fr_ref[1:2, :] * a)
    h = jnp.dot(a, w3_ref[...], precision=hi, preferred_element_type=F32)
    window = jnp.exp(-t_ref[...] * jnp.exp(dec_ref[...]))
    row = lax.broadcasted_iota(jnp.int32, h.shape, 0) + i * tm
    h = jnp.where(row < n_valid, h * window, 0.0)
    h_ref[...] = h.astype(h_ref.dtype)

    @pl.when(i == 0)
    def _():
        ssq_ref[...] = jnp.zeros_like(ssq_ref)
        h0_ref[...] = jnp.broadcast_to(h[0:1, :], h0_ref.shape)

    ssq_ref[...] += jnp.broadcast_to(jnp.sum(h * h, axis=0, keepdims=True), ssq_ref.shape)


def hyena_filter(zfeat, tcol, w1, b1, w2, b2, w3, freq, log_decay, n_valid, tm=512, tc=2048):
    P = zfeat.shape[0]
    FH, NC = w3.shape
    tc = _tile(NC, tc)
    return pl.pallas_call(
        functools.partial(_filter_body, n_valid=n_valid),
        grid=(NC // tc, P // tm),
        in_specs=[pl.BlockSpec((tm, LANE), lambda c, i: (i, 0)),
                  pl.BlockSpec((LANE, FH), lambda c, i: (0, 0)),
                  pl.BlockSpec((1, FH), lambda c, i: (0, 0)),
                  pl.BlockSpec((FH, FH), lambda c, i: (0, 0)),
                  pl.BlockSpec((1, FH), lambda c, i: (0, 0)),
                  pl.BlockSpec((FH, tc), lambda c, i: (0, c)),
                  pl.BlockSpec((2, FH), lambda c, i: (0, 0)),
                  pl.BlockSpec((1, tc), lambda c, i: (0, c)),
                  pl.BlockSpec((tm, 1), lambda c, i: (i, 0))],
        out_specs=[pl.BlockSpec((tm, tc), lambda c, i: (i, c)),
                   pl.BlockSpec((SUBLANE, tc), lambda c, i: (0, c)),
                   pl.BlockSpec((SUBLANE, tc), lambda c, i: (0, c))],
        out_shape=[jax.ShapeDtypeStruct((P, NC), F32),
                   jax.ShapeDtypeStruct((SUBLANE, NC), F32),
                   jax.ShapeDtypeStruct((SUBLANE, NC), F32)],
        compiler_params=_cparams("parallel", "arbitrary"),
        name="hyena_filter",
    )(zfeat, w1, b1.reshape(1, FH), w2, b2.reshape(1, FH), w3, freq, log_decay.reshape(1, NC), tcol)


U32 = jnp.uint32


def _pack_complex(y):
    return pltpu.bitcast(y.astype(BF16), U32)


def _unpack_complex(u):
    return pltpu.bitcast(u, BF16)


def _fft_fwd_body(x_ref, g_ref, y_ref):
    for j in range(FFT_JB):
        xj = x_ref[:, j, :].astype(BF16)
        y_ref[j] = _pack_complex(jnp.dot(g_ref[j], xj, preferred_element_type=F32))


def fft_fwd(x3, col0, ncols, gfwd, td=512):
    R, C, _ = x3.shape
    n1 = gfwd.shape[1] // 2
    td = _tile(ncols, td)
    c0 = col0 // td
    return pl.pallas_call(
        _fft_fwd_body,
        grid=(C // FFT_JB, ncols // td),
        in_specs=[pl.BlockSpec((R, FFT_JB, td), lambda i, j: (0, i, c0 + j)),
                  pl.BlockSpec((FFT_JB, 2 * n1, R), lambda i, j: (i, 0, 0))],
        out_specs=pl.BlockSpec((FFT_JB, n1, td), lambda i, j: (i, 0, j)),
        out_shape=jax.ShapeDtypeStruct((C, n1, ncols), U32),
        compiler_params=_cparams("parallel", "parallel"),
        name="fft_fwd",
    )(x3, gfwd)


def _filter_spec_body(ya_ref, yb_ref, wf_ref, s_ref, k_ref):
    c = ya_ref.shape[0]
    s = s_ref[...]
    for kk in range(FFT_KB):
        sa = jnp.dot(wf_ref[...], _unpack_complex(ya_ref[:, kk, :]), preferred_element_type=F32)
        sb = jnp.dot(wf_ref[...], _unpack_complex(yb_ref[:, kk, :]), preferred_element_type=F32)
        k_ref[kk, 0] = ((sa[:c] + sb[:c]) * s).astype(k_ref.dtype)
        k_ref[kk, 1] = ((sa[c:] - sb[c:]) * s).astype(k_ref.dtype)


def filter_spectrum(yh, order, D, wf, scale, td=512):
    C, n1, _ = yh.shape
    td = _tile(D, td)
    ca = (order * 2 * D) // td
    cb = (order * 2 * D + D) // td
    return pl.pallas_call(
        _filter_spec_body,
        grid=(n1 // FFT_KB, D // td),
        in_specs=[pl.BlockSpec((C, FFT_KB, td), lambda k, j: (0, k, ca + j)),
                  pl.BlockSpec((C, FFT_KB, td), lambda k, j: (0, k, cb + j)),
                  pl.BlockSpec((2 * C, 2 * C), lambda k, j: (0, 0)),
                  pl.BlockSpec((1, td), lambda k, j: (0, j))],
        out_specs=pl.BlockSpec((FFT_KB, 2, C, td), lambda k, j: (k, 0, 0, j)),
        out_shape=jax.ShapeDtypeStruct((n1, 2, C, D), BF16),
        compiler_params=_cparams("parallel", "parallel"),
        name="filter_spectrum",
    )(yh, yh, wf, scale)


def _spectral_body(y_ref, k_ref, wf_ref, wi_ref, o_ref):
    c = y_ref.shape[0]
    for kk in range(FFT_KB):
        x = jnp.dot(wf_ref[...], _unpack_complex(y_ref[:, kk, :]), preferred_element_type=F32)
        xr, xi = x[:c], x[c:]
        kr, ki = k_ref[kk, 0].astype(F32), k_ref[kk, 1].astype(F32)
        z = jnp.concatenate([xr * kr - xi * ki, xr * ki + xi * kr], axis=0).astype(BF16)
        o_ref[kk] = _pack_complex(jnp.dot(wi_ref[...], z, preferred_element_type=F32))


def spectral_mul(y, k, wf, wi, td=1024):
    C, n1, D = y.shape
    td = _tile(D, td)
    mat = pl.BlockSpec((2 * C, 2 * C), lambda a, j: (0, 0))
    return pl.pallas_call(
        _spectral_body,
        grid=(n1 // FFT_KB, D // td),
        in_specs=[pl.BlockSpec((C, FFT_KB, td), lambda a, j: (0, a, j)),
                  pl.BlockSpec((FFT_KB, 2, C, td), lambda a, j: (a, 0, 0, j)),
                  mat, mat],
        out_specs=pl.BlockSpec((FFT_KB, C, td), lambda a, j: (a, 0, j)),
        out_shape=jax.ShapeDtypeStruct((n1, C, D), U32),
        compiler_params=_cparams("parallel", "parallel"),
        name="spectral_mul",
    )(y, k, wf, wi)


def _fft_inv_body(y_ref, gi_ref, z_ref, gate_ref, skip_ref, o_ref, *, n_valid):
    i = pl.program_id(0)
    R = z_ref.shape[0]
    n1_idx = lax.broadcasted_iota(jnp.int32, (R, z_ref.shape[2]), 0)
    for j in range(FFT_JB):
        conv = jnp.dot(gi_ref[j], _unpack_complex(y_ref[:, j, :]), preferred_element_type=F32)
        z = gate_ref[:, j, :] * (conv + z_ref[:, j, :] * skip_ref[...])
        seq = n1_idx * FFT_C + (i * FFT_JB + j)
        o_ref[:, j, :] = jnp.where(seq < n_valid, z, 0.0)


def fft_inv_gate(y, ginv, z3, zcol0, gate3, gcol0, skip, n_valid, td=512):
    n1, C, D = y.shape
    R = ginv.shape[1]
    td = _tile(D, td)
    zc, gc = zcol0 // td, gcol0 // td
    return pl.pallas_call(
        functools.partial(_fft_inv_body, n_valid=n_valid),
        grid=(C // FFT_JB, D // td),
        in_specs=[pl.BlockSpec((n1, FFT_JB, td), lambda i, j: (0, i, j)),
                  pl.BlockSpec((FFT_JB, R, 2 * n1), lambda i, j: (i, 0, 0)),
                  pl.BlockSpec((R, FFT_JB, td), lambda i, j: (0, i, zc + j)),
                  pl.BlockSpec((R, FFT_JB, td), lambda i, j: (0, i, gc + j)),
                  pl.BlockSpec((1, td), lambda i, j: (0, j))],
        out_specs=pl.BlockSpec((R, FFT_JB, td), lambda i, j: (0, i, j)),
        out_shape=jax.ShapeDtypeStruct((R, C, D), F32),
        compiler_params=_cparams("parallel", "parallel"),
        name="fft_inv_gate",
    )(y, ginv, z3, gate3, skip)


def _dft_tables(R):
    C = FFT_C
    n1c = 2 * R
    N = n1c * C
    k1 = jnp.arange(n1c, dtype=jnp.int32)[None, :, None]
    n1 = jnp.arange(R, dtype=jnp.int32)[None, None, :]
    n2 = jnp.arange(C, dtype=jnp.int32)[:, None, None]
    ang = (2.0 * math.pi / N) * ((k1 * (C * n1 + n2)) % N).astype(F32)
    gr, gim = jnp.cos(ang), -jnp.sin(ang)
    gfwd = jnp.stack([gr, gim], axis=2).reshape(C, 2 * n1c, R).astype(BF16)
    ginv = gfwd.transpose(0, 2, 1)
    a = jnp.arange(C, dtype=jnp.int32)
    angc = (2.0 * math.pi / C) * ((a[:, None] * a[None, :]) % C).astype(F32)
    fr, fi = jnp.cos(angc), -jnp.sin(angc)
    wf = jnp.block([[fr, -fi], [fi, fr]]).reshape(2 * C, 2, C).transpose(0, 2, 1).reshape(2 * C, 2 * C)
    wi = jnp.block([[fr, fi], [-fi, fr]]).reshape(2, C, 2 * C).transpose(1, 0, 2).reshape(2 * C, 2 * C)
    return gfwd, ginv, wf.astype(BF16), wi.astype(BF16), N


def _pos_features(L, P):
    bands = (POS_EMB_DIM - 1) // 2
    i = jnp.arange(P, dtype=F32)
    t = jnp.linspace(0.0, 1.0, L, dtype=F32)
    t = jnp.concatenate([t, jnp.ones((P - L,), F32)])[:, None]
    w = (2.0 * math.pi / L) * i[:, None]
    f = jnp.linspace(1e-4, bands - 1, bands, dtype=F32)[None, :]
    z = jnp.concatenate([t, jnp.cos(f * w), -jnp.sin(f * w)], axis=-1)
    return jnp.pad(z, ((0, 0), (0, LANE - POS_EMB_DIM))), t


def hyena_group(uc, L, D, filt, skip):
    P = uc.shape[0]
    R = P // FFT_C
    w1p, b1, w2, b2, w3f, freq, dec = filt
    gfwd, ginv, wf, wi, N = _dft_tables(R)
    zfeat, tcol = _pos_features(L, P)
    hw, ssq, h0 = hyena_filter(zfeat, tcol, w1p, b1, w2, b2, w3f, freq, dec, L)
    n_orders = skip.shape[0]
    ssq = ssq[0].reshape(n_orders, 2, D)
    h0 = h0[0].reshape(n_orders, 2, D)
    total = ssq[:, 0] + ssq[:, 1] + 2.0 * h0[:, 0] * h0[:, 1]
    scale = lax.rsqrt(total + EPS) / N
    yh = fft_fwd(hw.reshape(R, FFT_C, hw.shape[1]), 0, hw.shape[1], gfwd)
    uc3 = uc.reshape(R, FFT_C, 3 * D)
    z3, zcol = uc3, 0
    for n in range(n_orders):
        kspec = filter_spectrum(yh, n, D, wf, scale[n:n + 1])
        y = fft_fwd(z3, zcol, D, gfwd)
        y = spectral_mul(y, kspec, wf, wi)
        z3 = fft_inv_gate(y, ginv, z3, zcol, uc3, (n + 1) * D, skip[n:n + 1], L)
        zcol = 0
    return z3.reshape(P, D)


def _rope(x, rot_ref, cos_ref, sin_ref):
    xr = jnp.dot(x.astype(BF16), rot_ref[...], preferred_element_type=F32)
    return x * cos_ref[...] + xr * sin_ref[...]


def _q_head_body(c_ref, gq_ref, w_ref, gh_ref, rot_ref, cos_ref, sin_ref, q_ref, cn_ref, *, q_lora, scale):
    @pl.when(pl.program_id(1) == 0)
    def _():
        cn_ref[...] = _rms(c_ref[:, :q_lora], gq_ref[...]).astype(BF16)

    q = jnp.dot(cn_ref[...], w_ref[0], preferred_element_type=F32)
    q = _rope(_rms(q, gh_ref[...]), rot_ref, cos_ref, sin_ref)
    q_ref[0] = (q * scale).astype(q_ref.dtype)


def mla_q_heads(cqkv, g_q, w_uq_h, g_qhead, rot, cosf, sinf, q_lora, tm=TOK_TILE):
    T, CW = cqkv.shape
    H = w_uq_h.shape[0]
    return pl.pallas_call(
        functools.partial(_q_head_body, q_lora=q_lora, scale=QK_HEAD ** -0.5 * math.log2(math.e)),
        grid=(T // tm, H),
        in_specs=[pl.BlockSpec((tm, CW), lambda i, h: (i, 0)),
                  pl.BlockSpec((1, q_lora), lambda i, h: (0, 0)),
                  pl.BlockSpec((1, q_lora, QK_HEAD), lambda i, h: (h, 0, 0)),
                  pl.BlockSpec((1, QK_HEAD), lambda i, h: (0, 0)),
                  pl.BlockSpec((QK_HEAD, QK_HEAD), lambda i, h: (0, 0)),
                  pl.BlockSpec((tm, QK_HEAD), lambda i, h: (i, 0)),
                  pl.BlockSpec((tm, QK_HEAD), lambda i, h: (i, 0))],
        out_specs=pl.BlockSpec((1, tm, QK_HEAD), lambda i, h: (h, i, 0)),
        out_shape=jax.ShapeDtypeStruct((H, T, QK_HEAD), BF16),
        scratch_shapes=[pltpu.VMEM((tm, q_lora), BF16)],
        compiler_params=_cparams("parallel", "arbitrary"),
        name="mla_q_heads",
    )(cqkv, g_q.reshape(1, -1), w_uq_h, g_qhead.reshape(1, -1), rot, cosf, sinf)


def _kv_head_body(c_ref, gkv_ref, wk_ref, wvt_ref, gh_ref, rot_ref, cos_ref, sin_ref, k_ref, vt_ref, cn_ref,
                  *, q_lora, kv_lora):
    @pl.when(pl.program_id(1) == 0)
    def _():
        cn_ref[...] = _rms(c_ref[:, q_lora:q_lora + kv_lora], gkv_ref[...]).astype(BF16)

    cn = cn_ref[...]
    k_nope = jnp.dot(cn, wk_ref[0], preferred_element_type=F32)
    k_pe = c_ref[:, q_lora + kv_lora:q_lora + kv_lora + QK_ROPE]
    k = jnp.concatenate([k_nope, k_pe], axis=-1)
    k = _rope(_rms(k, gh_ref[...]), rot_ref, cos_ref, sin_ref)
    k_ref[0] = k.astype(k_ref.dtype)
    vt = lax.dot_general(wvt_ref[0], cn, (((1,), (1,)), ((), ())), preferred_element_type=F32)
    vt_ref[0, :V_HEAD, :] = vt.astype(vt_ref.dtype)
    extra = vt_ref.shape[1] - V_HEAD
    row = lax.broadcasted_iota(jnp.int32, (extra, vt.shape[1]), 0)
    vt_ref[0, V_HEAD:, :] = jnp.where(row == 0, 1.0, 0.0).astype(vt_ref.dtype)


def mla_kv_heads(cqkv, g_kv, w_uk_h, w_uvt_h, g_khead, rot, cosf, sinf, q_lora, kv_lora, tm=TOK_TILE):
    T, CW = cqkv.shape
    H = w_uk_h.shape[0]
    return pl.pallas_call(
        functools.partial(_kv_head_body, q_lora=q_lora, kv_lora=kv_lora),
        grid=(T // tm, H),
        in_specs=[pl.BlockSpec((tm, CW), lambda i, h: (i, 0)),
                  pl.BlockSpec((1, kv_lora), lambda i, h: (0, 0)),
                  pl.BlockSpec((1, kv_lora, QK_NOPE), lambda i, h: (h, 0, 0)),
                  pl.BlockSpec((1, V_HEAD, kv_lora), lambda i, h: (h, 0, 0)),
                  pl.BlockSpec((1, QK_HEAD), lambda i, h: (0, 0)),
                  pl.BlockSpec((QK_HEAD, QK_HEAD), lambda i, h: (0, 0)),
                  pl.BlockSpec((tm, QK_HEAD), lambda i, h: (i, 0)),
                  pl.BlockSpec((tm, QK_HEAD), lambda i, h: (i, 0))],
        out_specs=[pl.BlockSpec((1, tm, QK_HEAD), lambda i, h: (h, i, 0)),
                   pl.BlockSpec((1, VT_ROWS, tm), lambda i, h: (h, 0, i))],
        out_shape=[jax.ShapeDtypeStruct((H, T, QK_HEAD), BF16),
                   jax.ShapeDtypeStruct((H, VT_ROWS, T), BF16)],
        scratch_shapes=[pltpu.VMEM((tm, kv_lora), BF16)],
        compiler_params=_cparams("parallel", "arbitrary"),
        name="mla_kv_heads",
    )(cqkv, g_kv.reshape(1, -1), w_uk_h, w_uvt_h, g_khead.reshape(1, -1), rot, cosf, sinf)


def _flash_body(q_ref, k_hbm, vt_hbm, o_prev_ref, o_ref, k_buf, vt_buf, acc_ref, sem,
                *, row0, n_full, tail, tail_valid, tk):
    del o_prev_ref
    h = pl.program_id(0)
    n_keys = k_buf.shape[0]

    @pl.when(pl.program_id(1) == 0)
    def _():
        ck = pltpu.make_async_copy(k_hbm.at[h, pl.ds(row0, n_keys)], k_buf, sem.at[0])
        cv = pltpu.make_async_copy(vt_hbm.at[h, :, pl.ds(row0, n_keys)], vt_buf, sem.at[1])
        ck.start()
        cv.start()
        ck.wait()
        cv.wait()

    q = q_ref[0]
    tq = q.shape[0]
    acc_ref[...] = jnp.zeros_like(acc_ref)

    def chunk(start, size, m, n_valid=None):
        s = lax.dot_general(k_buf[pl.ds(start, size), :], q, (((1,), (1,)), ((), ())),
                            preferred_element_type=F32)
        if n_valid is not None:
            s = jnp.where(lax.broadcasted_iota(jnp.int32, s.shape, 0) < n_valid, s, NEG_BIG)
        m_new = jnp.maximum(m, jnp.max(s, axis=0, keepdims=True))
        alpha = jnp.exp2(m - m_new)
        p = jnp.exp2(s - m_new).astype(BF16)
        pv = jnp.dot(vt_buf[:, pl.ds(start, size)], p, preferred_element_type=F32)
        acc_ref[...] = alpha * acc_ref[...] + pv
        return m_new

    m = jnp.full((1, tq), NEG_BIG, F32)
    m = lax.fori_loop(0, n_full, lambda j, m: chunk(pl.multiple_of(j * tk, tk), tk, m), m, unroll=2)
    if tail:
        m = chunk(n_full * tk, tail, m, tail_valid)
    acc = acc_ref[...]
    o = acc[:V_HEAD] / acc[V_HEAD:V_HEAD + 1]
    o_ref[...] = o.T.astype(o_ref.dtype)


def flash_attention(q, k, vt, o_prev, row0, n_rows, n_valid, tq=512, tk=512):
    H, T, _ = q.shape
    q0 = row0 // tq
    n_full = n_valid // tk
    tail_valid = n_valid - n_full * tk
    tail = -(-tail_valid // LANE) * LANE
    return pl.pallas_call(
        functools.partial(_flash_body, row0=row0, n_full=n_full, tail=tail, tail_valid=tail_valid, tk=tk),
        grid=(H, n_rows // tq),
        in_specs=[pl.BlockSpec((1, tq, QK_HEAD), lambda h, i: (h, q0 + i, 0)),
                  pl.BlockSpec(memory_space=pl.ANY),
                  pl.BlockSpec(memory_space=pl.ANY),
                  pl.BlockSpec(memory_space=pl.ANY)],
        out_specs=pl.BlockSpec((tq, V_HEAD), lambda h, i: (q0 + i, h)),
        out_shape=jax.ShapeDtypeStruct((T, H * V_HEAD), BF16),
        scratch_shapes=[pltpu.VMEM((n_rows, QK_HEAD), BF16), pltpu.VMEM((VT_ROWS, n_rows), BF16),
                        pltpu.VMEM((VT_ROWS, tq), F32), pltpu.SemaphoreType.DMA((2,))],
        input_output_aliases={3: 0},
        compiler_params=_cparams("arbitrary", "arbitrary"),
        name="flash_attention",
    )(q, k, vt, o_prev)


META_E1, META_E2, META_G1, META_G2, META_R1, META_R2 = range(6)


def _router_body(h_ref, g_ref, wr_ref, xn_ref, meta_ref, cnt_ref, carry_ref):
    i = pl.program_id(0)

    @pl.when(i == 0)
    def _():
        carry_ref[...] = jnp.zeros_like(carry_ref)

    xn = _rms(h_ref[...], g_ref[...])
    xn_ref[...] = xn
    logits = jnp.dot(xn, wr_ref[...], precision=lax.Precision.HIGHEST, preferred_element_type=F32)
    tb = logits.shape[0]
    lane = lax.broadcasted_iota(jnp.int32, logits.shape, 1)
    lg = jnp.where(lane < N_EXPERTS, logits, NEG_BIG)
    m1 = jnp.max(lg, axis=-1, keepdims=True)
    i1 = jnp.min(jnp.where(lg == m1, lane, LANE), axis=-1, keepdims=True)
    lg2 = jnp.where(lane == i1, NEG_BIG, lg)
    m2 = jnp.max(lg2, axis=-1, keepdims=True)
    i2 = jnp.min(jnp.where(lg2 == m2, lane, LANE), axis=-1, keepdims=True)
    e = jnp.exp(m2 - m1)
    g1 = 1.0 / (1.0 + e)
    g2 = e / (1.0 + e)
    oh = jnp.where((lane == i1) | (lane == i2), 1.0, 0.0)
    r = lax.broadcasted_iota(jnp.int32, (tb, tb), 0)
    c = lax.broadcasted_iota(jnp.int32, (tb, tb), 1)
    tri = jnp.where(c < r, 1.0, 0.0).astype(BF16)
    pre = jnp.dot(tri, oh.astype(BF16), preferred_element_type=F32) + carry_ref[0:1, :]
    r1 = jnp.sum(jnp.where(lane == i1, pre, 0.0), axis=-1, keepdims=True)
    r2 = jnp.sum(jnp.where(lane == i2, pre, 0.0), axis=-1, keepdims=True)
    carry_ref[...] += jnp.broadcast_to(jnp.sum(oh, axis=0, keepdims=True), carry_ref.shape)
    meta = jnp.zeros(logits.shape, F32)
    for idx, val in ((META_E1, i1.astype(F32)), (META_E2, i2.astype(F32)), (META_G1, g1), (META_G2, g2),
                     (META_R1, r1), (META_R2, r2)):
        meta = jnp.where(lane == idx, val, meta)
    meta_ref[...] = meta
    cnt_ref[...] = carry_ref[...]


def moe_router(h, g, w_router_pad, tb=256):
    T, D = h.shape
    return pl.pallas_call(
        _router_body,
        grid=(T // tb,),
        in_specs=[pl.BlockSpec((tb, D), lambda i: (i, 0)),
                  pl.BlockSpec((1, D), lambda i: (0, 0)),
                  pl.BlockSpec((D, LANE), lambda i: (0, 0))],
        out_specs=[pl.BlockSpec((tb, D), lambda i: (i, 0)),
                   pl.BlockSpec((tb, LANE), lambda i: (i, 0)),
                   pl.BlockSpec((SUBLANE, LANE), lambda i: (0, 0))],
        out_shape=[jax.ShapeDtypeStruct((T, D), F32),
                   jax.ShapeDtypeStruct((T, LANE), F32),
                   jax.ShapeDtypeStruct((SUBLANE, LANE), F32)],
        scratch_shapes=[pltpu.VMEM((SUBLANE, LANE), F32)],
        compiler_params=_cparams("arbitrary"),
        name="moe_router",
    )(h, g.reshape(1, D), w_router_pad)


def _expert_up_body(te_ref, nu_ref, src_ref, xn_hbm, wg_ref, wu_ref, o_ref, xf_ref, xb_ref, sem):
    active = pl.program_id(0) < nu_ref[0]
    tm = xf_ref.shape[0]

    @pl.when(active & (pl.program_id(1) == 0))
    def _():
        def row_copy(r):
            return pltpu.make_async_copy(xn_hbm.at[pl.ds(src_ref[0, 0, r], 1)], xf_ref.at[pl.ds(r, 1)], sem)

        def start(r, c):
            row_copy(r).start()
            return c

        def wait(r, c):
            row_copy(r).wait()
            return c

        lax.fori_loop(0, tm, start, 0)
        lax.fori_loop(0, tm, wait, 0)
        xb_ref[...] = xf_ref[...].astype(BF16)

    @pl.when(active)
    def _():
        x = xb_ref[...]
        a = jnp.dot(x, wg_ref[0], preferred_element_type=F32)
        b = jnp.dot(x, wu_ref[0], preferred_element_type=F32)
        o_ref[...] = _silu_mul(a, b).astype(o_ref.dtype)

    @pl.when(jnp.logical_not(active))
    def _():
        o_ref[...] = jnp.zeros_like(o_ref)


def expert_up(xn, src, wg, wu, tile_expert, n_used, tm, tf=1024):
    D = xn.shape[1]
    n_tiles = src.shape[0] // tm
    F = wg.shape[2]
    tf = _tile(F, tf)
    grid_spec = pltpu.PrefetchScalarGridSpec(
        num_scalar_prefetch=2,
        grid=(n_tiles, F // tf),
        in_specs=[pl.BlockSpec((1, 1, tm), lambda i, j, te, nu: (i, 0, 0), memory_space=pltpu.SMEM),
                  pl.BlockSpec(memory_space=pl.ANY),
                  pl.BlockSpec((1, D, tf), lambda i, j, te, nu: (te[i], 0, j)),
                  pl.BlockSpec((1, D, tf), lambda i, j, te, nu: (te[i], 0, j))],
        out_specs=pl.BlockSpec((tm, tf), lambda i, j, te, nu: (i, j)),
        scratch_shapes=[pltpu.VMEM((tm, D), F32), pltpu.VMEM((tm, D), BF16), pltpu.SemaphoreType.DMA(())],
    )
    return pl.pallas_call(
        _expert_up_body,
        grid_spec=grid_spec,
        out_shape=jax.ShapeDtypeStruct((n_tiles * tm, F), BF16),
        compiler_params=_cparams("arbitrary", "arbitrary"),
        name="moe_expert_up",
    )(tile_expert, n_used, src.reshape(n_tiles, 1, tm), xn, wg, wu)


def _expert_down_body(te_ref, nu_ref, a_ref, w_ref, o_ref):
    @pl.when(pl.program_id(0) < nu_ref[0])
    def _():
        o_ref[...] = jnp.dot(a_ref[...], w_ref[0], preferred_element_type=F32)

    @pl.when(pl.program_id(0) >= nu_ref[0])
    def _():
        o_ref[...] = jnp.zeros_like(o_ref)


def expert_down(act, wd, tile_expert, n_used, tm, tn=512):
    S, F = act.shape
    D = wd.shape[2]
    tn = _tile(D, tn)
    grid_spec = pltpu.PrefetchScalarGridSpec(
        num_scalar_prefetch=2,
        grid=(S // tm, D // tn),
        in_specs=[pl.BlockSpec((tm, F), lambda i, j, te, nu: (i, 0)),
                  pl.BlockSpec((1, F, tn), lambda i, j, te, nu: (te[i], 0, j))],
        out_specs=pl.BlockSpec((tm, tn), lambda i, j, te, nu: (i, j)),
    )
    return pl.pallas_call(
        _expert_down_body,
        grid_spec=grid_spec,
        out_shape=jax.ShapeDtypeStruct((S, D), F32),
        compiler_params=_cparams("parallel", "parallel"),
        name="moe_expert_down",
    )(tile_expert, n_used, act, wd)


def _combine_body(pos_ref, h_ref, meta_ref, ys_hbm, o_ref, buf, sem, *, tb):
    def row_copy(r, k):
        return pltpu.make_async_copy(ys_hbm.at[pl.ds(pos_ref[0, 0, 2 * r + k], 1)],
                                     buf.at[k, pl.ds(r, 1)], sem)

    def start(r, c):
        row_copy(r, 0).start()
        row_copy(r, 1).start()
        return c

    def wait(r, c):
        row_copy(r, 0).wait()
        row_copy(r, 1).wait()
        return c

    lax.fori_loop(0, tb, start, 0)
    lax.fori_loop(0, tb, wait, 0)
    g1 = meta_ref[:, META_G1:META_G1 + 1]
    g2 = meta_ref[:, META_G2:META_G2 + 1]
    o_ref[...] = h_ref[...] + g1 * buf[0] + g2 * buf[1]


def moe_combine(h, meta, pos, ys, tb=256):
    T, D = h.shape
    pos3 = pos.reshape(T // tb, 1, 2 * tb)
    return pl.pallas_call(
        functools.partial(_combine_body, tb=tb),
        grid=(T // tb,),
        in_specs=[pl.BlockSpec((1, 1, 2 * tb), lambda i: (i, 0, 0), memory_space=pltpu.SMEM),
                  pl.BlockSpec((tb, D), lambda i: (i, 0)),
                  pl.BlockSpec((tb, LANE), lambda i: (i, 0)),
                  pl.BlockSpec(memory_space=pl.ANY)],
        out_specs=pl.BlockSpec((tb, D), lambda i: (i, 0)),
        out_shape=jax.ShapeDtypeStruct((T, D), F32),
        scratch_shapes=[pltpu.VMEM((2, tb, D), F32), pltpu.SemaphoreType.DMA(())],
        compiler_params=_cparams("arbitrary"),
        name="moe_combine",
    )(pos3, h, meta, ys)


def moe_layer(h, g, w_router, wg, wu, wd, tm=512):
    T, D = h.shape
    wr = jnp.pad(w_router, ((0, 0), (0, LANE - N_EXPERTS)))
    xn, meta, cnt = moe_router(h, g, wr)
    counts = cnt[0, :N_EXPERTS].astype(jnp.int32)
    tiles = (counts + tm - 1) // tm
    tile_end = jnp.cumsum(tiles)
    starts = (tile_end - tiles) * tm
    n_tiles = (2 * T + N_EXPERTS * (tm - 1)) // tm
    e_idx = meta[:, META_E1:META_E2 + 1].astype(jnp.int32)
    rank = meta[:, META_R1:META_R2 + 1].astype(jnp.int32)
    pos = starts[e_idx] + rank
    tile_expert = jnp.minimum(jnp.sum(jnp.arange(n_tiles)[:, None] >= tile_end[None, :], axis=1),
                              N_EXPERTS - 1).astype(jnp.int32)
    n_used = tile_end[-1:].astype(jnp.int32)
    tok = jnp.repeat(jnp.arange(T, dtype=jnp.int32), 2)
    src = jnp.zeros((n_tiles * tm,), jnp.int32).at[pos.reshape(-1)].set(tok)
    act = expert_up(xn, src, wg, wu, tile_expert, n_used, tm)
    ys = expert_down(act, wd, tile_expert, n_used, tm)
    return moe_combine(h, meta, pos, ys)


def _padded_len(L):
    r = -(-L // FFT_C)
    r = -(-r // SUBLANE) * SUBLANE
    return r * FFT_C


def _rope_tables(lens, pads):
    half = QK_ROPE // 2
    inv = 1.0 / (ROPE_BASE ** (jnp.arange(0, QK_ROPE, 2, dtype=F32) / QK_ROPE))
    cs, ss = [], []
    for L, P in zip(lens, pads):
        ang = jnp.arange(P, dtype=F32)[:, None] * inv[None, :]
        c, s = jnp.cos(ang), jnp.sin(ang)
        cs.append(jnp.concatenate([jnp.ones((P, QK_NOPE), F32), c, c], axis=-1))
        ss.append(jnp.concatenate([jnp.zeros((P, QK_NOPE), F32), s, s], axis=-1))
    rot = np.zeros((QK_HEAD, QK_HEAD), np.float32)
    for c in range(QK_NOPE, QK_NOPE + half):
        rot[c + half, c] = -1.0
        rot[c, c + half] = 1.0
    return jnp.concatenate(cs), jnp.concatenate(ss), jnp.asarray(rot, BF16)


def kernel(x_prompt, x_sample, meta_tokens, norm_mix, norm_ffn, hy_w_in, hy_conv_w, hy_conv_b, hy_filt_w1, hy_filt_b1, hy_filt_w2, hy_filt_b2, hy_filt_w3, hy_filt_freq, hy_filt_log_decay, hy_skip, hy_w_out, ffn_w_gate, ffn_w_up, ffn_w_down, mla_w_dq, mla_g_q, mla_w_uq, mla_w_dkv, mla_g_kv, mla_w_ukv, mla_g_qhead, mla_g_khead, mla_w_o, moe_w_router, moe_w_gate, moe_w_up, moe_w_down):
    D = x_prompt.shape[-1]
    n_meta = meta_tokens.shape[0]
    assert x_prompt.shape[0] == 1 and x_sample.shape[0] == 1
    groups = (x_prompt[0], x_sample[0])
    lens = [n_meta + g.shape[0] for g in groups]
    pads = [_padded_len(L) for L in lens]
    row0s = [0, pads[0]]
    T = sum(pads)
    assert T % TOK_TILE == 0 and all(p % TOK_TILE == 0 for p in pads)

    parts = []
    for g, L, P in zip(groups, lens, pads):
        parts += [meta_tokens.astype(F32), g, jnp.zeros((P - L, D), F32)]
    h = jnp.concatenate(parts, axis=0)

    bf = lambda w: w.astype(BF16)

    u = norm_mm(h, norm_mix[0], [bf(hy_w_in)], lambda y: y, BF16, 1024, "hyena_in_proj")
    n_orders = hy_skip.shape[0]
    fh = hy_filt_w2.shape[0]
    filt = (jnp.pad(hy_filt_w1, ((0, LANE - POS_EMB_DIM), (0, 0))), hy_filt_b1, hy_filt_w2, hy_filt_b2,
            hy_filt_w3.reshape(fh, n_orders * 2 * D), hy_filt_freq, hy_filt_log_decay.reshape(-1))
    zs = []
    for L, P, r0 in zip(lens, pads, row0s):
        uc = short_conv(u, hy_conv_w, hy_conv_b, r0, P, L)
        zs.append(hyena_group(uc, L, D, filt, hy_skip))
    z = jnp.concatenate(zs, axis=0)
    h = mm_res(z, bf(hy_w_out), h, 1024, "hyena_out_proj")

    a = norm_mm(h, norm_ffn[0], [bf(ffn_w_gate), bf(ffn_w_up)], _silu_mul, BF16, 1024, "ffn_swiglu")
    h = mm_res(a, bf(ffn_w_down), h, 512, "ffn_down")

    q_lora = mla_w_dq.shape[1]
    kv_lora = mla_w_ukv.shape[0]
    cw = q_lora + mla_w_dkv.shape[1]
    cw_pad = -(-cw // LANE) * LANE
    w_down = jnp.pad(jnp.concatenate([mla_w_dq, mla_w_dkv], axis=1), ((0, 0), (0, cw_pad - cw)))
    cqkv = norm_mm(h, norm_mix[1], [bf(w_down)], lambda y: y, F32, cw_pad, "mla_down_proj")
    cosf, sinf, rot = _rope_tables(lens, pads)
    w_uq_h = bf(mla_w_uq.reshape(q_lora, MLA_HEADS, QK_HEAD).transpose(1, 0, 2))
    w_ukv_h = mla_w_ukv.reshape(kv_lora, MLA_HEADS, QK_NOPE + V_HEAD).transpose(1, 0, 2)
    w_uk_h = bf(w_ukv_h[:, :, :QK_NOPE])
    w_uvt_h = bf(w_ukv_h[:, :, QK_NOPE:].transpose(0, 2, 1))
    q = mla_q_heads(cqkv, mla_g_q, w_uq_h, mla_g_qhead, rot, cosf, sinf, q_lora)
    k, vt = mla_kv_heads(cqkv, mla_g_kv, w_uk_h, w_uvt_h, mla_g_khead, rot, cosf, sinf, q_lora, kv_lora)
    o = jnp.zeros((T, MLA_HEADS * V_HEAD), BF16)
    for L, P, r0 in zip(lens, pads, row0s):
        o = flash_attention(q, k, vt, o, r0, P, L)
    h = mm_res(o, bf(mla_w_o), h, 1024, "mla_out_proj")

    h = moe_layer(h, norm_ffn[1], moe_w_router, bf(moe_w_gate), bf(moe_w_up), bf(moe_w_down))

    outs = []
    for g, r0 in zip(groups, row0s):
        outs.append(h[r0 + n_meta:r0 + n_meta + g.shape[0]][None])
    return tuple(outs)
```

```python
import functools
import math

import numpy as np
import jax
import jax.numpy as jnp
from jax import lax
from jax.experimental import pallas as pl
from jax.experimental.pallas import tpu as pltpu

F32 = jnp.float32
BF16 = jnp.bfloat16

EPS = 1e-6
MLA_HEADS = 16
QK_NOPE = 128
QK_ROPE = 64
QK_HEAD = QK_NOPE + QK_ROPE
V_HEAD = 128
VT_ROWS = V_HEAD + 16
ROPE_BASE = 10000.0
N_EXPERTS = 8
POS_EMB_DIM = 33

LANE = 128
SUBLANE = 8
VMEM_LIMIT_BYTES = 56 * 1024 * 1024

FFT_C = 128
FFT_JB = SUBLANE
FFT_KB = SUBLANE

TOK_TILE = 512
DMA_UNROLL = 8
NEG_BIG = -1e30


def _cparams(*sem):
    return pltpu.CompilerParams(dimension_semantics=sem, vmem_limit_bytes=VMEM_LIMIT_BYTES)


def _tile(n, pref):
    if n <= pref:
        return n
    t = pref - pref % LANE
    while n % t:
        t -= LANE
    return t


def _rms(x, g):
    ms = jnp.mean(x * x, axis=-1, keepdims=True)
    return x * lax.rsqrt(ms + EPS) * g


def _norm_mm_body(x_ref, g_ref, *rest, n_w, epilogue):
    w_refs, o_ref, xn_ref = rest[:n_w], rest[n_w], rest[n_w + 1]

    @pl.when(pl.program_id(1) == 0)
    def _():
        xn_ref[...] = _rms(x_ref[...], g_ref[...]).astype(BF16)

    xn = xn_ref[...]
    ys = [jnp.dot(xn, w[...], preferred_element_type=F32) for w in w_refs]
    o_ref[...] = epilogue(*ys).astype(o_ref.dtype)


def norm_mm(x, g, ws, epilogue, out_dtype, tn, name, tm=TOK_TILE):
    T, K = x.shape
    N = ws[0].shape[1]
    tn = _tile(N, tn)
    n_w = len(ws)
    return pl.pallas_call(
        functools.partial(_norm_mm_body, n_w=n_w, epilogue=epilogue),
        grid=(T // tm, N // tn),
        in_specs=[pl.BlockSpec((tm, K), lambda i, j: (i, 0)),
                  pl.BlockSpec((1, K), lambda i, j: (0, 0))]
        + [pl.BlockSpec((K, tn), lambda i, j: (0, j))] * n_w,
        out_specs=pl.BlockSpec((tm, tn), lambda i, j: (i, j)),
        out_shape=jax.ShapeDtypeStruct((T, N), out_dtype),
        scratch_shapes=[pltpu.VMEM((tm, K), BF16)],
        compiler_params=_cparams("parallel", "arbitrary"),
        name=name,
    )(x, g.reshape(1, K), *ws)


def _mm_res_body(x_ref, w_ref, r_ref, o_ref):
    o_ref[...] = r_ref[...] + jnp.dot(x_ref[...].astype(BF16), w_ref[...], preferred_element_type=F32)


def mm_res(x, w, res, tn, name, tm=TOK_TILE):
    T, K = x.shape
    N = w.shape[1]
    tn = _tile(N, tn)
    return pl.pallas_call(
        _mm_res_body,
        grid=(T // tm, N // tn),
        in_specs=[pl.BlockSpec((tm, K), lambda i, j: (i, 0)),
                  pl.BlockSpec((K, tn), lambda i, j: (0, j)),
                  pl.BlockSpec((tm, tn), lambda i, j: (i, j))],
        out_specs=pl.BlockSpec((tm, tn), lambda i, j: (i, j)),
        out_shape=jax.ShapeDtypeStruct((T, N), F32),
        compiler_params=_cparams("parallel", "parallel"),
        name=name,
    )(x, w, res)


def _silu_mul(a, b):
    return a * (1.0 / (1.0 + jnp.exp(-a))) * b


def _short_conv_body(u_ref, prev_ref, next_ref, w_ref, b_ref, o_ref, *, row0, n_valid, n_rows_total, halo):
    i = pl.program_id(0)
    tm = u_ref.shape[0]
    u = u_ref[...].astype(F32)
    row = lax.broadcasted_iota(jnp.int32, u.shape, 0)
    g_row = row0 + i * tm
    prev_row = jnp.where(g_row == 0, 0.0, prev_ref[...].astype(F32)[halo - 1:halo, :])
    next_row = jnp.where(g_row + tm == n_rows_total, 0.0, next_ref[...].astype(F32)[0:1, :])
    up = jnp.where(row == 0, prev_row, pltpu.roll(u, 1, 0))
    dn = jnp.where(row == tm - 1, next_row, pltpu.roll(u, tm - 1, 0))
    y = b_ref[...] + up * w_ref[0:1, :] + u * w_ref[1:2, :] + dn * w_ref[2:3, :]
    o_ref[...] = jnp.where(i * tm + row < n_valid, y, 0.0)


def short_conv(u, w, b, row0, n_rows, n_valid, tm=TOK_TILE, tn=1024):
    T, C3 = u.shape
    tn = _tile(C3, tn)
    halo = 16
    r0 = row0 // tm
    hb = tm // halo
    nhalo = T // halo
    return pl.pallas_call(
        functools.partial(_short_conv_body, row0=row0, n_valid=n_valid, n_rows_total=T, halo=halo),
        grid=(n_rows // tm, C3 // tn),
        in_specs=[pl.BlockSpec((tm, tn), lambda i, j: (r0 + i, j)),
                  pl.BlockSpec((halo, tn), lambda i, j: (jnp.maximum((r0 + i) * hb - 1, 0), j)),
                  pl.BlockSpec((halo, tn), lambda i, j: (jnp.minimum((r0 + i + 1) * hb, nhalo - 1), j)),
                  pl.BlockSpec((3, tn), lambda i, j: (0, j)),
                  pl.BlockSpec((1, tn), lambda i, j: (0, j))],
        out_specs=pl.BlockSpec((tm, tn), lambda i, j: (i, j)),
        out_shape=jax.ShapeDtypeStruct((n_rows, C3), F32),
        compiler_params=_cparams("parallel", "parallel"),
        name="hyena_short_conv",
    )(u, u, u, w, b.reshape(1, C3))


def _filter_body(z_ref, w1_ref, b1_ref, w2_ref, b2_ref, w3_ref, fr_ref, dec_ref, t_ref,
                 h_ref, ssq_ref, hdn_ref, *, n_valid):
    i = pl.program_id(0)
    tm = z_ref.shape[0]

    @pl.when(pl.program_id(1) == 0)
    def _():
        hi = lax.Precision.HIGHEST
        a = jnp.dot(z_ref[...], w1_ref[...], precision=hi, preferred_element_type=F32) + b1_ref[...]
        a = jnp.sin(fr_ref[0:1, :] * a)
        a = jnp.dot(a, w2_ref[...], precision=hi, preferred_element_type=F32) + b2_ref[...]
        hdn_ref[...] = jnp.sin(fr_ref[1:2, :] * a).astype(BF16)

    h = jnp.dot(hdn_ref[...], w3_ref[...], preferred_element_type=F32)
    window = jnp.exp(-t_ref[...] * jnp.exp(dec_ref[...]))
    row = lax.broadcasted_iota(jnp.int32, h.shape, 0) + i * tm
    h = jnp.where(row < n_valid, h * window, 0.0)
    h_ref[...] = h
    ssq_ref[...] = jnp.broadcast_to(jnp.sum(h * h, axis=0, keepdims=True), ssq_ref.shape)


def hyena_filter(zfeat, tcol, w1, b1, w2, b2, w3, freq, log_decay, n_valid, tm=512, tc=2048):
    P = zfeat.shape[0]
    FH, NC = w3.shape
    tc = _tile(NC, tc)
    return pl.pallas_call(
        functools.partial(_filter_body, n_valid=n_valid),
        grid=(P // tm, NC // tc),
        in_specs=[pl.BlockSpec((tm, LANE), lambda i, c: (i, 0)),
                  pl.BlockSpec((LANE, FH), lambda i, c: (0, 0)),
                  pl.BlockSpec((1, FH), lambda i, c: (0, 0)),
                  pl.BlockSpec((FH, FH), lambda i, c: (0, 0)),
                  pl.BlockSpec((1, FH), lambda i, c: (0, 0)),
                  pl.BlockSpec((FH, tc), lambda i, c: (0, c)),
                  pl.BlockSpec((2, FH), lambda i, c: (0, 0)),
                  pl.BlockSpec((1, tc), lambda i, c: (0, c)),
                  pl.BlockSpec((tm, 1), lambda i, c: (i, 0))],
        out_specs=[pl.BlockSpec((tm, tc), lambda i, c: (i, c)),
                   pl.BlockSpec((SUBLANE, tc), lambda i, c: (i, c))],
        out_shape=[jax.ShapeDtypeStruct((P, NC), F32),
                   jax.ShapeDtypeStruct((P // tm * SUBLANE, NC), F32)],
        scratch_shapes=[pltpu.VMEM((tm, FH), BF16)],
        compiler_params=_cparams("parallel", "arbitrary"),
        name="hyena_filter",
    )(zfeat, w1, b1.reshape(1, FH), w2, b2.reshape(1, FH), w3, freq, log_decay.reshape(1, NC), tcol)


U32 = jnp.uint32


def _pack_complex(y):
    return pltpu.bitcast(y.astype(BF16), U32)


def _unpack_complex(u):
    return pltpu.bitcast(u, BF16)


def _fft_fwd_body(x_ref, g_ref, y_ref):
    for j in range(FFT_JB):
        xj = x_ref[:, j, :].astype(BF16)
        y_ref[j] = _pack_complex(jnp.dot(g_ref[j], xj, preferred_element_type=F32))


def fft_fwd(x3, col0, ncols, gfwd, td=512):
    R, C, _ = x3.shape
    n1 = gfwd.shape[1] // 2
    td = _tile(ncols, td)
    c0 = col0 // td
    return pl.pallas_call(
        _fft_fwd_body,
        grid=(C // FFT_JB, ncols // td),
        in_specs=[pl.BlockSpec((R, FFT_JB, td), lambda i, j: (0, i, c0 + j)),
                  pl.BlockSpec((FFT_JB, 2 * n1, R), lambda i, j: (i, 0, 0))],
        out_specs=pl.BlockSpec((FFT_JB, n1, td), lambda i, j: (i, 0, j)),
        out_shape=jax.ShapeDtypeStruct((C, n1, ncols), U32),
        compiler_params=_cparams("parallel", "parallel"),
        name="fft_fwd",
    )(x3, gfwd)


def _spectral_body(y_ref, ya_ref, yb_ref, s_ref, wf_ref, wi_ref, o_ref):
    c = y_ref.shape[0]
    s = s_ref[...]
    for kk in range(FFT_KB):
        x = jnp.dot(wf_ref[...], _unpack_complex(y_ref[:, kk, :]), preferred_element_type=F32)
        sa = jnp.dot(wf_ref[...], _unpack_complex(ya_ref[:, kk, :]), preferred_element_type=F32)
        sb = jnp.dot(wf_ref[...], _unpack_complex(yb_ref[:, kk, :]), preferred_element_type=F32)
        xr, xi = x[:c], x[c:]
        kr, ki = (sa[:c] + sb[:c]) * s, (sa[c:] - sb[c:]) * s
        z = jnp.concatenate([xr * kr - xi * ki, xr * ki + xi * kr], axis=0).astype(BF16)
        o_ref[kk] = _pack_complex(jnp.dot(wi_ref[...], z, preferred_element_type=F32))


def spectral_mul(y, yh, order, scale, wf, wi, td=512):
    C, n1, D = y.shape
    td = _tile(D, td)
    ca = (order * 2 * D) // td
    cb = (order * 2 * D + D) // td
    mat = pl.BlockSpec((2 * C, 2 * C), lambda a, j: (0, 0))
    return pl.pallas_call(
        _spectral_body,
        grid=(n1 // FFT_KB, D // td),
        in_specs=[pl.BlockSpec((C, FFT_KB, td), lambda a, j: (0, a, j)),
                  pl.BlockSpec((C, FFT_KB, td), lambda a, j: (0, a, ca + j)),
                  pl.BlockSpec((C, FFT_KB, td), lambda a, j: (0, a, cb + j)),
                  pl.BlockSpec((1, td), lambda a, j: (0, j)),
                  mat, mat],
        out_specs=pl.BlockSpec((FFT_KB, C, td), lambda a, j: (a, 0, j)),
        out_shape=jax.ShapeDtypeStruct((n1, C, D), U32),
        compiler_params=_cparams("parallel", "parallel"),
        name="spectral_mul",
    )(y, yh, yh, scale, wf, wi)


def _fft_inv_body(y_ref, gi_ref, z_ref, gate_ref, skip_ref, o_ref, *, n_valid):
    i = pl.program_id(0)
    R = z_ref.shape[0]
    n1_idx = lax.broadcasted_iota(jnp.int32, (R, z_ref.shape[2]), 0)
    for j in range(FFT_JB):
        conv = jnp.dot(gi_ref[j], _unpack_complex(y_ref[:, j, :]), preferred_element_type=F32)
        z = gate_ref[:, j, :] * (conv + z_ref[:, j, :] * skip_ref[...])
        seq = n1_idx * FFT_C + (i * FFT_JB + j)
        o_ref[:, j, :] = jnp.where(seq < n_valid, z, 0.0)


def fft_inv_gate(y, ginv, z3, zcol0, gate3, gcol0, skip, n_valid, td=512):
    n1, C, D = y.shape
    R = ginv.shape[1]
    td = _tile(D, td)
    zc, gc = zcol0 // td, gcol0 // td
    return pl.pallas_call(
        functools.partial(_fft_inv_body, n_valid=n_valid),
        grid=(C // FFT_JB, D // td),
        in_specs=[pl.BlockSpec((n1, FFT_JB, td), lambda i, j: (0, i, j)),
                  pl.BlockSpec((FFT_JB, R, 2 * n1), lambda i, j: (i, 0, 0)),
                  pl.BlockSpec((R, FFT_JB, td), lambda i, j: (0, i, zc + j)),
                  pl.BlockSpec((R, FFT_JB, td), lambda i, j: (0, i, gc + j)),
                  pl.BlockSpec((1, td), lambda i, j: (0, j))],
        out_specs=pl.BlockSpec((R, FFT_JB, td), lambda i, j: (0, i, j)),
        out_shape=jax.ShapeDtypeStruct((R, C, D), F32),
        compiler_params=_cparams("parallel", "parallel"),
        name="fft_inv_gate",
    )(y, ginv, z3, gate3, skip)


def _dft_tables(R):
    C = FFT_C
    n1c = 2 * R
    N = n1c * C
    k1 = jnp.arange(n1c, dtype=jnp.int32)[None, :, None]
    n1 = jnp.arange(R, dtype=jnp.int32)[None, None, :]
    n2 = jnp.arange(C, dtype=jnp.int32)[:, None, None]
    ang = (2.0 * math.pi / N) * ((k1 * (C * n1 + n2)) % N).astype(F32)
    gr, gim = jnp.cos(ang), -jnp.sin(ang)
    gfwd = jnp.stack([gr, gim], axis=2).reshape(C, 2 * n1c, R).astype(BF16)
    ginv = gfwd.transpose(0, 2, 1)
    a = jnp.arange(C, dtype=jnp.int32)
    angc = (2.0 * math.pi / C) * ((a[:, None] * a[None, :]) % C).astype(F32)
    fr, fi = jnp.cos(angc), -jnp.sin(angc)
    wf = jnp.block([[fr, -fi], [fi, fr]]).reshape(2 * C, 2, C).transpose(0, 2, 1).reshape(2 * C, 2 * C)
    wi = jnp.block([[fr, fi], [-fi, fr]]).reshape(2, C, 2 * C).transpose(1, 0, 2).reshape(2 * C, 2 * C)
    return gfwd, ginv, wf.astype(BF16), wi.astype(BF16), N


def _pos_features(L, P):
    bands = (POS_EMB_DIM - 1) // 2
    i = jnp.arange(P, dtype=F32)
    t = jnp.linspace(0.0, 1.0, L, dtype=F32)
    t = jnp.concatenate([t, jnp.ones((P - L,), F32)])[:, None]
    w = (2.0 * math.pi / L) * i[:, None]
    f = jnp.linspace(1e-4, bands - 1, bands, dtype=F32)[None, :]
    z = jnp.concatenate([t, jnp.cos(f * w), -jnp.sin(f * w)], axis=-1)
    return jnp.pad(z, ((0, 0), (0, LANE - POS_EMB_DIM))), t


def hyena_group(uc, L, D, filt, skip):
    P = uc.shape[0]
    R = P // FFT_C
    w1p, b1, w2, b2, w3f, freq, dec = filt
    gfwd, ginv, wf, wi, N = _dft_tables(R)
    zfeat, tcol = _pos_features(L, P)
    hw, ssq = hyena_filter(zfeat, tcol, w1p, b1, w2, b2, w3f, freq, dec, L)
    n_orders = skip.shape[0]
    ssq = jnp.sum(ssq[::SUBLANE], axis=0).reshape(n_orders, 2, D)
    h0 = hw[0].reshape(n_orders, 2, D)
    total = ssq[:, 0] + ssq[:, 1] + 2.0 * h0[:, 0] * h0[:, 1]
    scale = lax.rsqrt(total + EPS) / N
    yh = fft_fwd(hw.reshape(R, FFT_C, hw.shape[1]), 0, hw.shape[1], gfwd)
    uc3 = uc.reshape(R, FFT_C, 3 * D)
    z3, zcol = uc3, 0
    for n in range(n_orders):
        y = fft_fwd(z3, zcol, D, gfwd)
        y = spectral_mul(y, yh, n, scale[n:n + 1], wf, wi)
        z3 = fft_inv_gate(y, ginv, z3, zcol, uc3, (n + 1) * D, skip[n:n + 1], L)
        zcol = 0
    return z3.reshape(P, D)


def _rope(x, rot_ref, cos_ref, sin_ref):
    xr = jnp.dot(x.astype(BF16), rot_ref[...], preferred_element_type=F32)
    return x * cos_ref[...] + xr * sin_ref[...]


def _q_head_body(c_ref, gq_ref, w_ref, gh_ref, rot_ref, cos_ref, sin_ref, q_ref, cn_ref, *, q_lora, scale):
    @pl.when(pl.program_id(1) == 0)
    def _():
        cn_ref[...] = _rms(c_ref[:, :q_lora], gq_ref[...]).astype(BF16)

    q = jnp.dot(cn_ref[...], w_ref[0], preferred_element_type=F32)
    q = _rope(_rms(q, gh_ref[...]), rot_ref, cos_ref, sin_ref)
    q_ref[0] = (q * scale).astype(q_ref.dtype)


def mla_q_heads(cqkv, g_q, w_uq_h, g_qhead, rot, cosf, sinf, q_lora, tm=TOK_TILE):
    T, CW = cqkv.shape
    H = w_uq_h.shape[0]
    return pl.pallas_call(
        functools.partial(_q_head_body, q_lora=q_lora, scale=QK_HEAD ** -0.5 * math.log2(math.e)),
        grid=(T // tm, H),
        in_specs=[pl.BlockSpec((tm, CW), lambda i, h: (i, 0)),
                  pl.BlockSpec((1, q_lora), lambda i, h: (0, 0)),
                  pl.BlockSpec((1, q_lora, QK_HEAD), lambda i, h: (h, 0, 0)),
                  pl.BlockSpec((1, QK_HEAD), lambda i, h: (0, 0)),
                  pl.BlockSpec((QK_HEAD, QK_HEAD), lambda i, h: (0, 0)),
                  pl.BlockSpec((tm, QK_HEAD), lambda i, h: (i, 0)),
                  pl.BlockSpec((tm, QK_HEAD), lambda i, h: (i, 0))],
        out_specs=pl.BlockSpec((1, tm, QK_HEAD), lambda i, h: (h, i, 0)),
        out_shape=jax.ShapeDtypeStruct((H, T, QK_HEAD), BF16),
        scratch_shapes=[pltpu.VMEM((tm, q_lora), BF16)],
        compiler_params=_cparams("parallel", "arbitrary"),
        name="mla_q_heads",
    )(cqkv, g_q.reshape(1, -1), w_uq_h, g_qhead.reshape(1, -1), rot, cosf, sinf)


def _kv_head_body(c_ref, gkv_ref, wk_ref, wvt_ref, gh_ref, rot_ref, cos_ref, sin_ref, k_ref, vt_ref, cn_ref,
                  *, q_lora, kv_lora):
    @pl.when(pl.program_id(1) == 0)
    def _():
        cn_ref[...] = _rms(c_ref[:, q_lora:q_lora + kv_lora], gkv_ref[...]).astype(BF16)

    cn = cn_ref[...]
    k_nope = jnp.dot(cn, wk_ref[0], preferred_element_type=F32)
    k_pe = c_ref[:, q_lora + kv_lora:q_lora + kv_lora + QK_ROPE]
    k = jnp.concatenate([k_nope, k_pe], axis=-1)
    k = _rope(_rms(k, gh_ref[...]), rot_ref, cos_ref, sin_ref)
    k_ref[0] = k.astype(k_ref.dtype)
    vt = lax.dot_general(wvt_ref[0], cn, (((1,), (1,)), ((), ())), preferred_element_type=F32)
    vt_ref[0, :V_HEAD, :] = vt.astype(vt_ref.dtype)
    extra = vt_ref.shape[1] - V_HEAD
    row = lax.broadcasted_iota(jnp.int32, (extra, vt.shape[1]), 0)
    vt_ref[0, V_HEAD:, :] = jnp.where(row == 0, 1.0, 0.0).astype(vt_ref.dtype)


def mla_kv_heads(cqkv, g_kv, w_uk_h, w_uvt_h, g_khead, rot, cosf, sinf, q_lora, kv_lora, tm=TOK_TILE):
    T, CW = cqkv.shape
    H = w_uk_h.shape[0]
    return pl.pallas_call(
        functools.partial(_kv_head_body, q_lora=q_lora, kv_lora=kv_lora),
        grid=(T // tm, H),
        in_specs=[pl.BlockSpec((tm, CW), lambda i, h: (i, 0)),
                  pl.BlockSpec((1, kv_lora), lambda i, h: (0, 0)),
                  pl.BlockSpec((1, kv_lora, QK_NOPE), lambda i, h: (h, 0, 0)),
                  pl.BlockSpec((1, V_HEAD, kv_lora), lambda i, h: (h, 0, 0)),
                  pl.BlockSpec((1, QK_HEAD), lambda i, h: (0, 0)),
                  pl.BlockSpec((QK_HEAD, QK_HEAD), lambda i, h: (0, 0)),
                  pl.BlockSpec((tm, QK_HEAD), lambda i, h: (i, 0)),
                  pl.BlockSpec((tm, QK_HEAD), lambda i, h: (i, 0))],
        out_specs=[pl.BlockSpec((1, tm, QK_HEAD), lambda i, h: (h, i, 0)),
                   pl.BlockSpec((1, VT_ROWS, tm), lambda i, h: (h, 0, i))],
        out_shape=[jax.ShapeDtypeStruct((H, T, QK_HEAD), BF16),
                   jax.ShapeDtypeStruct((H, VT_ROWS, T), BF16)],
        scratch_shapes=[pltpu.VMEM((tm, kv_lora), BF16)],
        compiler_params=_cparams("parallel", "arbitrary"),
        name="mla_kv_heads",
    )(cqkv, g_kv.reshape(1, -1), w_uk_h, w_uvt_h, g_khead.reshape(1, -1), rot, cosf, sinf)


def _flash_body(q_ref, k_hbm, vt_hbm, o_prev_ref, o_ref, k_buf, vt_buf, acc_ref, s_buf, p_buf, sem,
                *, row0, n_chunks, last_valid, tk):
    del o_prev_ref
    h = pl.program_id(0)
    n_keys = k_buf.shape[0]

    @pl.when(pl.program_id(1) == 0)
    def _():
        ck = pltpu.make_async_copy(k_hbm.at[h, pl.ds(row0, n_keys)], k_buf, sem.at[0])
        cv = pltpu.make_async_copy(vt_hbm.at[h, :, pl.ds(row0, n_keys)], vt_buf, sem.at[1])
        ck.start()
        cv.start()
        ck.wait()
        cv.wait()

    q = q_ref[0]
    tq = q.shape[0]
    acc_ref[...] = jnp.zeros_like(acc_ref)

    def key_start(c):
        return c * tk if isinstance(c, int) else pl.multiple_of(c * tk, tk)

    def qk(c, slot):
        s_buf[slot] = lax.dot_general(k_buf[pl.ds(key_start(c), tk), :], q, (((1,), (1,)), ((), ())),
                                      preferred_element_type=F32)

    def softmax(slot, m, n_valid=None):
        s = s_buf[slot]
        if n_valid is not None:
            s = jnp.where(lax.broadcasted_iota(jnp.int32, s.shape, 0) < n_valid, s, NEG_BIG)
        m_new = jnp.maximum(m, jnp.max(s, axis=0, keepdims=True))
        p_buf[slot] = jnp.exp2(s - m_new).astype(BF16)
        return m_new, jnp.exp2(m - m_new)

    def pv(c, slot, alpha):
        acc_ref[...] = alpha * acc_ref[...] + jnp.dot(vt_buf[:, pl.ds(key_start(c), tk)], p_buf[slot],
                                                      preferred_element_type=F32)

    def step(c, par, m, alpha):
        pv(c, par, alpha)
        m, alpha = softmax(1 - par, m)
        qk(c + 2, par)
        return m, alpha

    def pair(i, carry):
        m, alpha = step(2 * i, 0, *carry)
        return step(2 * i + 1, 1, m, alpha)

    n = n_chunks
    qk(0, 0)
    qk(1, 1)
    m, alpha = softmax(0, jnp.full((1, tq), NEG_BIG, F32))
    m, alpha = lax.fori_loop(0, (n - 2) // 2, pair, (m, alpha))
    if (n - 2) % 2:
        m, alpha = step(n - 3, 0, m, alpha)
    par = (n - 2) % 2
    pv(n - 2, par, alpha)
    m, alpha = softmax(1 - par, m, last_valid)
    pv(n - 1, 1 - par, alpha)
    acc = acc_ref[...]
    o = acc[:V_HEAD] / acc[V_HEAD:V_HEAD + 1]
    o_ref[...] = o.T.astype(o_ref.dtype)


def flash_attention(q, k, vt, o_prev, row0, n_rows, n_valid, tq=512, tk=512):
    H, T, _ = q.shape
    q0 = row0 // tq
    n_chunks = -(-n_valid // tk)
    last_valid = n_valid - (n_chunks - 1) * tk
    assert n_chunks >= 3 and n_chunks * tk <= n_rows
    return pl.pallas_call(
        functools.partial(_flash_body, row0=row0, n_chunks=n_chunks, last_valid=last_valid, tk=tk),
        grid=(H, n_rows // tq),
        in_specs=[pl.BlockSpec((1, tq, QK_HEAD), lambda h, i: (h, q0 + i, 0)),
                  pl.BlockSpec(memory_space=pl.ANY),
                  pl.BlockSpec(memory_space=pl.ANY),
                  pl.BlockSpec(memory_space=pl.ANY)],
        out_specs=pl.BlockSpec((tq, V_HEAD), lambda h, i: (q0 + i, h)),
        out_shape=jax.ShapeDtypeStruct((T, H * V_HEAD), BF16),
        scratch_shapes=[pltpu.VMEM((n_rows, QK_HEAD), BF16), pltpu.VMEM((VT_ROWS, n_rows), BF16),
                        pltpu.VMEM((VT_ROWS, tq), F32), pltpu.VMEM((2, tk, tq), F32), pltpu.VMEM((2, tk, tq), BF16),
                        pltpu.SemaphoreType.DMA((2,))],
        input_output_aliases={3: 0},
        compiler_params=_cparams("arbitrary", "arbitrary"),
        name="flash_attention",
    )(q, k, vt, o_prev)


META_E1, META_E2, META_G1, META_G2, META_R1, META_R2 = range(6)


def _router_body(h_ref, g_ref, wr_ref, xn_ref, meta_ref, cnt_ref, carry_ref, *, token_ranges):
    i = pl.program_id(0)

    @pl.when(i == 0)
    def _():
        carry_ref[...] = jnp.zeros_like(carry_ref)

    xn = _rms(h_ref[...], g_ref[...])
    xn_ref[...] = xn
    logits = jnp.dot(xn, wr_ref[...], precision=lax.Precision.HIGHEST, preferred_element_type=F32)
    tb = logits.shape[0]
    lane = lax.broadcasted_iota(jnp.int32, logits.shape, 1)
    lg = jnp.where(lane < N_EXPERTS, logits, NEG_BIG)
    m1 = jnp.max(lg, axis=-1, keepdims=True)
    i1 = jnp.min(jnp.where(lg == m1, lane, LANE), axis=-1, keepdims=True)
    lg2 = jnp.where(lane == i1, NEG_BIG, lg)
    m2 = jnp.max(lg2, axis=-1, keepdims=True)
    i2 = jnp.min(jnp.where(lg2 == m2, lane, LANE), axis=-1, keepdims=True)
    e = jnp.exp(m2 - m1)
    row = lax.broadcasted_iota(jnp.int32, (tb, 1), 0) + i * tb
    valid = functools.reduce(jnp.logical_or, [(row >= a) & (row < b) for a, b in token_ranges])
    g1 = jnp.where(valid, 1.0 / (1.0 + e), 0.0)
    g2 = jnp.where(valid, e / (1.0 + e), 0.0)
    oh = jnp.where(valid & ((lane == i1) | (lane == i2)), 1.0, 0.0)
    r = lax.broadcasted_iota(jnp.int32, (tb, tb), 0)
    c = lax.broadcasted_iota(jnp.int32, (tb, tb), 1)
    tri = jnp.where(c < r, 1.0, 0.0).astype(BF16)
    pre = jnp.dot(tri, oh.astype(BF16), preferred_element_type=F32) + carry_ref[0:1, :]
    r1 = jnp.sum(jnp.where(lane == i1, pre, 0.0), axis=-1, keepdims=True)
    r2 = jnp.sum(jnp.where(lane == i2, pre, 0.0), axis=-1, keepdims=True)
    carry_ref[...] += jnp.broadcast_to(jnp.sum(oh, axis=0, keepdims=True), carry_ref.shape)
    meta = jnp.zeros(logits.shape, F32)
    for idx, val in ((META_E1, i1.astype(F32)), (META_E2, i2.astype(F32)), (META_G1, g1), (META_G2, g2),
                     (META_R1, r1), (META_R2, r2)):
        meta = jnp.where(lane == idx, val, meta)
    meta_ref[...] = meta
    cnt_ref[...] = carry_ref[...]


def moe_router(h, g, w_router_pad, token_ranges, tb=256):
    T, D = h.shape
    return pl.pallas_call(
        functools.partial(_router_body, token_ranges=token_ranges),
        grid=(T // tb,),
        in_specs=[pl.BlockSpec((tb, D), lambda i: (i, 0)),
                  pl.BlockSpec((1, D), lambda i: (0, 0)),
                  pl.BlockSpec((D, LANE), lambda i: (0, 0))],
        out_specs=[pl.BlockSpec((tb, D), lambda i: (i, 0)),
                   pl.BlockSpec((tb, LANE), lambda i: (i, 0)),
                   pl.BlockSpec((SUBLANE, LANE), lambda i: (0, 0))],
        out_shape=[jax.ShapeDtypeStruct((T, D), F32),
                   jax.ShapeDtypeStruct((T, LANE), F32),
                   jax.ShapeDtypeStruct((SUBLANE, LANE), F32)],
        scratch_shapes=[pltpu.VMEM((SUBLANE, LANE), F32)],
        compiler_params=_cparams("arbitrary"),
        name="moe_router",
    )(h, g.reshape(1, D), w_router_pad)


def _expert_up_body(te_ref, nu_ref, src_ref, xn_hbm, wg_ref, wu_ref, o_ref, xf_ref, xb_ref, sem):
    active = pl.program_id(0) < nu_ref[0]
    tm = xf_ref.shape[0]

    @pl.when(active & (pl.program_id(1) == 0))
    def _():
        def row_copy(r):
            return pltpu.make_async_copy(xn_hbm.at[pl.ds(src_ref[0, 0, r], 1)], xf_ref.at[pl.ds(r, 1)], sem)

        def start(r, c):
            row_copy(r).start()
            return c

        def wait(r, c):
            row_copy(r).wait()
            return c

        lax.fori_loop(0, tm, start, 0, unroll=DMA_UNROLL)
        lax.fori_loop(0, tm, wait, 0, unroll=DMA_UNROLL)
        xb_ref[...] = xf_ref[...].astype(BF16)

    @pl.when(active)
    def _():
        x = xb_ref[...]
        a = jnp.dot(x, wg_ref[0], preferred_element_type=F32)
        b = jnp.dot(x, wu_ref[0], preferred_element_type=F32)
        o_ref[...] = _silu_mul(a, b).astype(o_ref.dtype)

    @pl.when(jnp.logical_not(active))
    def _():
        o_ref[...] = jnp.zeros_like(o_ref)


def expert_up(xn, src, wg, wu, tile_expert, n_used, tm, tf=1024):
    D = xn.shape[1]
    n_tiles = src.shape[0] // tm
    F = wg.shape[2]
    tf = _tile(F, tf)
    grid_spec = pltpu.PrefetchScalarGridSpec(
        num_scalar_prefetch=2,
        grid=(n_tiles, F // tf),
        in_specs=[pl.BlockSpec((1, 1, tm), lambda i, j, te, nu: (i, 0, 0), memory_space=pltpu.SMEM),
                  pl.BlockSpec(memory_space=pl.ANY),
                  pl.BlockSpec((1, D, tf), lambda i, j, te, nu: (te[i], 0, j)),
                  pl.BlockSpec((1, D, tf), lambda i, j, te, nu: (te[i], 0, j))],
        out_specs=pl.BlockSpec((tm, tf), lambda i, j, te, nu: (i, j)),
        scratch_shapes=[pltpu.VMEM((tm, D), F32), pltpu.VMEM((tm, D), BF16), pltpu.SemaphoreType.DMA(())],
    )
    return pl.pallas_call(
        _expert_up_body,
        grid_spec=grid_spec,
        out_shape=jax.ShapeDtypeStruct((n_tiles * tm, F), BF16),
        compiler_params=_cparams("arbitrary", "arbitrary"),
        name="moe_expert_up",
    )(tile_expert, n_used, src.reshape(n_tiles, 1, tm), xn, wg, wu)


def _expert_down_body(te_ref, nu_ref, a_ref, w_ref, o_ref):
    @pl.when(pl.program_id(0) < nu_ref[0])
    def _():
        o_ref[...] = jnp.dot(a_ref[...], w_ref[0], preferred_element_type=F32)

    @pl.when(pl.program_id(0) >= nu_ref[0])
    def _():
        o_ref[...] = jnp.zeros_like(o_ref)


def expert_down(act, wd, tile_expert, n_used, tm, tn=512):
    S, F = act.shape
    D = wd.shape[2]
    tn = _tile(D, tn)
    grid_spec = pltpu.PrefetchScalarGridSpec(
        num_scalar_prefetch=2,
        grid=(S // tm, D // tn),
        in_specs=[pl.BlockSpec((tm, F), lambda i, j, te, nu: (i, 0)),
                  pl.BlockSpec((1, F, tn), lambda i, j, te, nu: (te[i], 0, j))],
        out_specs=pl.BlockSpec((tm, tn), lambda i, j, te, nu: (i, j)),
    )
    return pl.pallas_call(
        _expert_down_body,
        grid_spec=grid_spec,
        out_shape=jax.ShapeDtypeStruct((S, D), F32),
        compiler_params=_cparams("parallel", "parallel"),
        name="moe_expert_down",
    )(tile_expert, n_used, act, wd)


def _combine_body(pos_ref, h_ref, meta_ref, ys_hbm, o_ref, buf, sem, *, tb):
    def row_copy(r, k):
        return pltpu.make_async_copy(ys_hbm.at[pl.ds(pos_ref[0, 0, 2 * r + k], 1)],
                                     buf.at[k, pl.ds(r, 1)], sem)

    def start(r, c):
        row_copy(r, 0).start()
        row_copy(r, 1).start()
        return c

    def wait(r, c):
        row_copy(r, 0).wait()
        row_copy(r, 1).wait()
        return c

    lax.fori_loop(0, tb, start, 0, unroll=DMA_UNROLL)
    lax.fori_loop(0, tb, wait, 0, unroll=DMA_UNROLL)
    g1 = meta_ref[:, META_G1:META_G1 + 1]
    g2 = meta_ref[:, META_G2:META_G2 + 1]
    o_ref[...] = h_ref[...] + g1 * buf[0] + g2 * buf[1]


def moe_combine(h, meta, pos, ys, tb=256):
    T, D = h.shape
    pos3 = pos.reshape(T // tb, 1, 2 * tb)
    return pl.pallas_call(
        functools.partial(_combine_body, tb=tb),
        grid=(T // tb,),
        in_specs=[pl.BlockSpec((1, 1, 2 * tb), lambda i: (i, 0, 0), memory_space=pltpu.SMEM),
                  pl.BlockSpec((tb, D), lambda i: (i, 0)),
                  pl.BlockSpec((tb, LANE), lambda i: (i, 0)),
                  pl.BlockSpec(memory_space=pl.ANY)],
        out_specs=pl.BlockSpec((tb, D), lambda i: (i, 0)),
        out_shape=jax.ShapeDtypeStruct((T, D), F32),
        scratch_shapes=[pltpu.VMEM((2, tb, D), F32), pltpu.SemaphoreType.DMA(())],
        compiler_params=_cparams("arbitrary"),
        name="moe_combine",
    )(pos3, h, meta, ys)


def moe_layer(h, g, w_router, wg, wu, wd, token_ranges, tm=512):
    T, D = h.shape
    wr = jnp.pad(w_router, ((0, 0), (0, LANE - N_EXPERTS)))
    xn, meta, cnt = moe_router(h, g, wr, token_ranges)
    counts = cnt[0, :N_EXPERTS].astype(jnp.int32)
    tiles = (counts + tm - 1) // tm
    tile_end = jnp.cumsum(tiles)
    starts = (tile_end - tiles) * tm
    n_tok = sum(b - a for a, b in token_ranges)
    n_tiles = (2 * n_tok + N_EXPERTS * (tm - 1)) // tm
    e_idx = meta[:, META_E1:META_E2 + 1].astype(jnp.int32)
    rank = meta[:, META_R1:META_R2 + 1].astype(jnp.int32)
    row = jnp.arange(T, dtype=jnp.int32)[:, None]
    valid = functools.reduce(jnp.logical_or, [(row >= a) & (row < b) for a, b in token_ranges])
    pos = starts[e_idx] + rank
    tile_expert = jnp.minimum(jnp.sum(jnp.arange(n_tiles)[:, None] >= tile_end[None, :], axis=1),
                              N_EXPERTS - 1).astype(jnp.int32)
    n_used = tile_end[-1:].astype(jnp.int32)
    tok = jnp.repeat(row[:, 0], 2)
    src = jnp.zeros((n_tiles * tm,), jnp.int32).at[jnp.where(valid, pos, n_tiles * tm).reshape(-1)].set(
        tok, mode="drop")
    act = expert_up(xn, src, wg, wu, tile_expert, n_used, tm)
    ys = expert_down(act, wd, tile_expert, n_used, tm)
    return moe_combine(h, meta, jnp.where(valid, pos, 0), ys)


def _padded_len(L):
    r = -(-L // FFT_C)
    r_align = TOK_TILE // FFT_C
    r = -(-r // r_align) * r_align
    return r * FFT_C


def _rope_tables(lens, pads):
    half = QK_ROPE // 2
    inv = 1.0 / (ROPE_BASE ** (jnp.arange(0, QK_ROPE, 2, dtype=F32) / QK_ROPE))
    cs, ss = [], []
    for L, P in zip(lens, pads):
        ang = jnp.arange(P, dtype=F32)[:, None] * inv[None, :]
        c, s = jnp.cos(ang), jnp.sin(ang)
        cs.append(jnp.concatenate([jnp.ones((P, QK_NOPE), F32), c, c], axis=-1))
        ss.append(jnp.concatenate([jnp.zeros((P, QK_NOPE), F32), s, s], axis=-1))
    rot = np.zeros((QK_HEAD, QK_HEAD), np.float32)
    for c in range(QK_NOPE, QK_NOPE + half):
        rot[c + half, c] = -1.0
        rot[c, c + half] = 1.0
    return jnp.concatenate(cs), jnp.concatenate(ss), jnp.asarray(rot, BF16)


def kernel(x_prompt, x_sample, meta_tokens, norm_mix, norm_ffn, hy_w_in, hy_conv_w, hy_conv_b, hy_filt_w1, hy_filt_b1, hy_filt_w2, hy_filt_b2, hy_filt_w3, hy_filt_freq, hy_filt_log_decay, hy_skip, hy_w_out, ffn_w_gate, ffn_w_up, ffn_w_down, mla_w_dq, mla_g_q, mla_w_uq, mla_w_dkv, mla_g_kv, mla_w_ukv, mla_g_qhead, mla_g_khead, mla_w_o, moe_w_router, moe_w_gate, moe_w_up, moe_w_down):
    D = x_prompt.shape[-1]
    n_meta = meta_tokens.shape[0]
    assert x_prompt.shape[0] == 1 and x_sample.shape[0] == 1
    groups = (x_prompt[0], x_sample[0])
    lens = [n_meta + g.shape[0] for g in groups]
    pads = [_padded_len(L) for L in lens]
    row0s = [0, pads[0]]
    T = sum(pads)
    assert T % TOK_TILE == 0 and all(p % TOK_TILE == 0 for p in pads)

    parts = []
    for g, L, P in zip(groups, lens, pads):
        parts += [meta_tokens.astype(F32), g, jnp.zeros((P - L, D), F32)]
    h = jnp.concatenate(parts, axis=0)

    bf = lambda w: w.astype(BF16)

    u = norm_mm(h, norm_mix[0], [bf(hy_w_in)], lambda y: y, BF16, 1024, "hyena_in_proj")
    n_orders = hy_skip.shape[0]
    fh = hy_filt_w2.shape[0]
    filt = (jnp.pad(hy_filt_w1, ((0, LANE - POS_EMB_DIM), (0, 0))), hy_filt_b1, hy_filt_w2, hy_filt_b2,
            bf(hy_filt_w3.reshape(fh, n_orders * 2 * D)), hy_filt_freq, hy_filt_log_decay.reshape(-1))
    zs = []
    for L, P, r0 in zip(lens, pads, row0s):
        uc = short_conv(u, hy_conv_w, hy_conv_b, r0, P, L)
        zs.append(hyena_group(uc, L, D, filt, hy_skip))
    z = jnp.concatenate(zs, axis=0)
    h = mm_res(z, bf(hy_w_out), h, 1024, "hyena_out_proj")

    a = norm_mm(h, norm_ffn[0], [bf(ffn_w_gate), bf(ffn_w_up)], _silu_mul, BF16, 1024, "ffn_swiglu")
    h = mm_res(a, bf(ffn_w_down), h, 512, "ffn_down")

    q_lora = mla_w_dq.shape[1]
    kv_lora = mla_w_ukv.shape[0]
    cw = q_lora + mla_w_dkv.shape[1]
    cw_pad = -(-cw // LANE) * LANE
    w_down = jnp.pad(jnp.concatenate([mla_w_dq, mla_w_dkv], axis=1), ((0, 0), (0, cw_pad - cw)))
    cqkv = norm_mm(h, norm_mix[1], [bf(w_down)], lambda y: y, F32, cw_pad, "mla_down_proj")
    cosf, sinf, rot = _rope_tables(lens, pads)
    w_uq_h = bf(mla_w_uq.reshape(q_lora, MLA_HEADS, QK_HEAD).transpose(1, 0, 2))
    w_ukv_h = mla_w_ukv.reshape(kv_lora, MLA_HEADS, QK_NOPE + V_HEAD).transpose(1, 0, 2)
    w_uk_h = bf(w_ukv_h[:, :, :QK_NOPE])
    w_uvt_h = bf(w_ukv_h[:, :, QK_NOPE:].transpose(0, 2, 1))
    q = mla_q_heads(cqkv, mla_g_q, w_uq_h, mla_g_qhead, rot, cosf, sinf, q_lora)
    k, vt = mla_kv_heads(cqkv, mla_g_kv, w_uk_h, w_uvt_h, mla_g_khead, rot, cosf, sinf, q_lora, kv_lora)
    o = jnp.zeros((T, MLA_HEADS * V_HEAD), BF16)
    for L, P, r0 in zip(lens, pads, row0s):
        o = flash_attention(q, k, vt, o, r0, P, L)
    h = mm_res(o, bf(mla_w_o), h, 1024, "mla_out_proj")

    token_ranges = tuple((r0, r0 + L) for r0, L in zip(row0s, lens))
    h = moe_layer(h, norm_ffn[1], moe_w_router, bf(moe_w_gate), bf(moe_w_up), bf(moe_w_down), token_ranges)

    outs = []
    for g, r0 in zip(groups, row0s):
        outs.append(h[r0 + n_meta:r0 + n_meta + g.shape[0]][None])
    return tuple(outs)
```

```python
import functools
import math

import numpy as np
import jax
import jax.numpy as jnp
from jax import lax
from jax.experimental import pallas as pl
from jax.experimental.pallas import tpu as pltpu

F32 = jnp.float32
BF16 = jnp.bfloat16

EPS = 1e-6
MLA_HEADS = 16
QK_NOPE = 128
QK_ROPE = 64
QK_HEAD = QK_NOPE + QK_ROPE
V_HEAD = 128
VT_ROWS = V_HEAD + 16
ROPE_BASE = 10000.0
N_EXPERTS = 8
POS_EMB_DIM = 33

LANE = 128
SUBLANE = 8
VMEM_LIMIT_BYTES = 56 * 1024 * 1024

FFT_C = 128
FFT_JB = SUBLANE
FFT_KB = SUBLANE

TOK_TILE = 512
DMA_UNROLL = 8
NEG_BIG = -1e30


def _cparams(*sem):
    return pltpu.CompilerParams(dimension_semantics=sem, vmem_limit_bytes=VMEM_LIMIT_BYTES)


def _tile(n, pref):
    if n <= pref:
        return n
    t = pref - pref % LANE
    while n % t:
        t -= LANE
    return t


def _rms(x, g):
    ms = jnp.mean(x * x, axis=-1, keepdims=True)
    return x * lax.rsqrt(ms + EPS) * g


def _norm_mm_body(x_ref, g_ref, *rest, n_w, epilogue):
    w_refs, o_ref, xn_ref = rest[:n_w], rest[n_w], rest[n_w + 1]

    @pl.when(pl.program_id(1) == 0)
    def _():
        xn_ref[...] = _rms(x_ref[...], g_ref[...]).astype(BF16)

    xn = xn_ref[...]
    ys = [jnp.dot(xn, w[...], preferred_element_type=F32) for w in w_refs]
    o_ref[...] = epilogue(*ys).astype(o_ref.dtype)


def norm_mm(x, g, ws, epilogue, out_dtype, tn, name, tm=TOK_TILE):
    T, K = x.shape
    N = ws[0].shape[1]
    tn = _tile(N, tn)
    n_w = len(ws)
    return pl.pallas_call(
        functools.partial(_norm_mm_body, n_w=n_w, epilogue=epilogue),
        grid=(T // tm, N // tn),
        in_specs=[pl.BlockSpec((tm, K), lambda i, j: (i, 0)),
                  pl.BlockSpec((1, K), lambda i, j: (0, 0))]
        + [pl.BlockSpec((K, tn), lambda i, j: (0, j))] * n_w,
        out_specs=pl.BlockSpec((tm, tn), lambda i, j: (i, j)),
        out_shape=jax.ShapeDtypeStruct((T, N), out_dtype),
        scratch_shapes=[pltpu.VMEM((tm, K), BF16)],
        compiler_params=_cparams("parallel", "arbitrary"),
        name=name,
    )(x, g.reshape(1, K), *ws)


def _mm_res_body(x_ref, w_ref, r_ref, o_ref):
    o_ref[...] = r_ref[...] + jnp.dot(x_ref[...].astype(BF16), w_ref[...], preferred_element_type=F32)


def mm_res(x, w, res, tn, name, tm=TOK_TILE):
    T, K = x.shape
    N = w.shape[1]
    tn = _tile(N, tn)
    return pl.pallas_call(
        _mm_res_body,
        grid=(T // tm, N // tn),
        in_specs=[pl.BlockSpec((tm, K), lambda i, j: (i, 0)),
                  pl.BlockSpec((K, tn), lambda i, j: (0, j)),
                  pl.BlockSpec((tm, tn), lambda i, j: (i, j))],
        out_specs=pl.BlockSpec((tm, tn), lambda i, j: (i, j)),
        out_shape=jax.ShapeDtypeStruct((T, N), F32),
        compiler_params=_cparams("parallel", "parallel"),
        name=name,
    )(x, w, res)


def _silu_mul(a, b):
    return a * (1.0 / (1.0 + jnp.exp(-a))) * b


def _short_conv_body(u_ref, prev_ref, next_ref, w_ref, b_ref, o_ref, *, row0, n_valid, n_rows_total, halo):
    i = pl.program_id(0)
    tm = u_ref.shape[0]
    u = u_ref[...].astype(F32)
    row = lax.broadcasted_iota(jnp.int32, u.shape, 0)
    g_row = row0 + i * tm
    prev_row = jnp.where(g_row == 0, 0.0, prev_ref[...].astype(F32)[halo - 1:halo, :])
    next_row = jnp.where(g_row + tm == n_rows_total, 0.0, next_ref[...].astype(F32)[0:1, :])
    up = jnp.where(row == 0, prev_row, pltpu.roll(u, 1, 0))
    dn = jnp.where(row == tm - 1, next_row, pltpu.roll(u, tm - 1, 0))
    y = b_ref[...] + up * w_ref[0:1, :] + u * w_ref[1:2, :] + dn * w_ref[2:3, :]
    o_ref[...] = jnp.where(i * tm + row < n_valid, y, 0.0)


def short_conv(u, w, b, row0, n_rows, n_valid, tm=TOK_TILE, tn=1024):
    T, C3 = u.shape
    tn = _tile(C3, tn)
    halo = 16
    r0 = row0 // tm
    hb = tm // halo
    nhalo = T // halo
    return pl.pallas_call(
        functools.partial(_short_conv_body, row0=row0, n_valid=n_valid, n_rows_total=T, halo=halo),
        grid=(n_rows // tm, C3 // tn),
        in_specs=[pl.BlockSpec((tm, tn), lambda i, j: (r0 + i, j)),
                  pl.BlockSpec((halo, tn), lambda i, j: (jnp.maximum((r0 + i) * hb - 1, 0), j)),
                  pl.BlockSpec((halo, tn), lambda i, j: (jnp.minimum((r0 + i + 1) * hb, nhalo - 1), j)),
                  pl.BlockSpec((3, tn), lambda i, j: (0, j)),
                  pl.BlockSpec((1, tn), lambda i, j: (0, j))],
        out_specs=pl.BlockSpec((tm, tn), lambda i, j: (i, j)),
        out_shape=jax.ShapeDtypeStruct((n_rows, C3), F32),
        compiler_params=_cparams("parallel", "parallel"),
        name="hyena_short_conv",
    )(u, u, u, w, b.reshape(1, C3))


def _filter_body(z_ref, w1_ref, b1_ref, w2_ref, b2_ref, w3_ref, fr_ref, dec_ref, t_ref,
                 h_ref, ssq_ref, hdn_ref, *, n_valid):
    i = pl.program_id(0)
    tm = z_ref.shape[0]

    @pl.when(pl.program_id(1) == 0)
    def _():
        hi = lax.Precision.HIGHEST
        a = jnp.dot(z_ref[...], w1_ref[...], precision=hi, preferred_element_type=F32) + b1_ref[...]
        a = jnp.sin(fr_ref[0:1, :] * a)
        a = jnp.dot(a, w2_ref[...], precision=hi, preferred_element_type=F32) + b2_ref[...]
        hdn_ref[...] = jnp.sin(fr_ref[1:2, :] * a).astype(BF16)

    h = jnp.dot(hdn_ref[...], w3_ref[...], preferred_element_type=F32)
    window = jnp.exp(-t_ref[...] * jnp.exp(dec_ref[...]))
    row = lax.broadcasted_iota(jnp.int32, h.shape, 0) + i * tm
    h = jnp.where(row < n_valid, h * window, 0.0)
    h_ref[...] = h
    ssq_ref[...] = jnp.broadcast_to(jnp.sum(h * h, axis=0, keepdims=True), ssq_ref.shape)


def hyena_filter(zfeat, tcol, w1, b1, w2, b2, w3, freq, log_decay, n_valid, tm=512, tc=2048):
    P = zfeat.shape[0]
    FH, NC = w3.shape
    tc = _tile(NC, tc)
    return pl.pallas_call(
        functools.partial(_filter_body, n_valid=n_valid),
        grid=(P // tm, NC // tc),
        in_specs=[pl.BlockSpec((tm, LANE), lambda i, c: (i, 0)),
                  pl.BlockSpec((LANE, FH), lambda i, c: (0, 0)),
                  pl.BlockSpec((1, FH), lambda i, c: (0, 0)),
                  pl.BlockSpec((FH, FH), lambda i, c: (0, 0)),
                  pl.BlockSpec((1, FH), lambda i, c: (0, 0)),
                  pl.BlockSpec((FH, tc), lambda i, c: (0, c)),
                  pl.BlockSpec((2, FH), lambda i, c: (0, 0)),
                  pl.BlockSpec((1, tc), lambda i, c: (0, c)),
                  pl.BlockSpec((tm, 1), lambda i, c: (i, 0))],
        out_specs=[pl.BlockSpec((tm, tc), lambda i, c: (i, c)),
                   pl.BlockSpec((SUBLANE, tc), lambda i, c: (i, c))],
        out_shape=[jax.ShapeDtypeStruct((P, NC), F32),
                   jax.ShapeDtypeStruct((P // tm * SUBLANE, NC), F32)],
        scratch_shapes=[pltpu.VMEM((tm, FH), BF16)],
        compiler_params=_cparams("parallel", "arbitrary"),
        name="hyena_filter",
    )(zfeat, w1, b1.reshape(1, FH), w2, b2.reshape(1, FH), w3, freq, log_decay.reshape(1, NC), tcol)


def _fft_stage1_body(x_ref, f_ref, y_ref):
    y = jnp.dot(f_ref[...], x_ref[...].astype(BF16), preferred_element_type=F32)
    y_ref[...] = y.reshape(y_ref.shape).astype(y_ref.dtype)


def fft_stage1(x, col0, ncols, f1, td=8192):
    P, dtot = x.shape
    R = P // FFT_C
    ka = f1.shape[0] // 2
    td = _tile(ncols, td)
    x2 = x.reshape(R, FFT_C * dtot)
    nblk, c0, cstride = ncols // td, col0 // td, dtot // td
    return pl.pallas_call(
        _fft_stage1_body,
        grid=(FFT_C, nblk),
        in_specs=[pl.BlockSpec((R, td), lambda s, j: (0, s * cstride + c0 + j)),
                  pl.BlockSpec((2 * ka, R), lambda s, j: (0, 0))],
        out_specs=pl.BlockSpec((2, ka, td), lambda s, j: (0, 0, s * nblk + j)),
        out_shape=jax.ShapeDtypeStruct((2, ka, FFT_C * ncols), BF16),
        compiler_params=_cparams("parallel", "parallel"),
        name="fft_stage1",
    )(x2, f1)


def _fft_stage2_body(y_ref, ya_ref, yb_ref, s_ref, mf_ref, mi_ref, o_ref):
    c = y_ref.shape[2]
    td = y_ref.shape[3]
    s = s_ref[...]
    for kk in range(FFT_KB):
        mf = mf_ref[kk]
        x = jnp.dot(mf, y_ref[:, kk].reshape(2 * c, td), preferred_element_type=F32)
        sa = jnp.dot(mf, ya_ref[:, kk].reshape(2 * c, td), preferred_element_type=F32)
        sb = jnp.dot(mf, yb_ref[:, kk].reshape(2 * c, td), preferred_element_type=F32)
        xr, xi = x[:c], x[c:]
        kr, ki = (sa[:c] + sb[:c]) * s, (sa[c:] - sb[c:]) * s
        z = jnp.concatenate([xr * kr - xi * ki, xr * ki + xi * kr], axis=0).astype(BF16)
        o = jnp.dot(mi_ref[kk], z, preferred_element_type=F32)
        o_ref[:, kk] = o.reshape(2, c, td).astype(o_ref.dtype)


def fft_stage2(y, yh, order, scale, mf, mi, td=1024):
    _, ka, cols = y.shape
    D = cols // FFT_C
    nc = yh.shape[2] // FFT_C
    td = _tile(D, td)
    ca = (order * 2 * D) // td
    cb = (order * 2 * D + D) // td
    y4 = y.reshape(2, ka, FFT_C, D)
    yh4 = yh.reshape(2, ka, FFT_C, nc)
    mat = pl.BlockSpec((FFT_KB, 2 * FFT_C, 2 * FFT_C), lambda a, j: (a, 0, 0))
    out = pl.pallas_call(
        _fft_stage2_body,
        grid=(ka // FFT_KB, D // td),
        in_specs=[pl.BlockSpec((2, FFT_KB, FFT_C, td), lambda a, j: (0, a, 0, j)),
                  pl.BlockSpec((2, FFT_KB, FFT_C, td), lambda a, j: (0, a, 0, ca + j)),
                  pl.BlockSpec((2, FFT_KB, FFT_C, td), lambda a, j: (0, a, 0, cb + j)),
                  pl.BlockSpec((1, td), lambda a, j: (0, j)),
                  mat, mat],
        out_specs=pl.BlockSpec((2, FFT_KB, FFT_C, td), lambda a, j: (0, a, 0, j)),
        out_shape=jax.ShapeDtypeStruct((2, ka, FFT_C, D), BF16),
        compiler_params=_cparams("parallel", "parallel"),
        name="fft_stage2",
    )(y4, yh4, yh4, scale, mf, mi)
    return out.reshape(2, ka, cols)


def _fft_stage3_body(y_ref, f_ref, z_ref, gate_ref, skip_ref, o_ref, *, n_valid):
    ka2, td = y_ref.shape[0] * y_ref.shape[1], y_ref.shape[2]
    conv = jnp.dot(f_ref[...], y_ref[...].reshape(ka2, td), preferred_element_type=F32)
    z = gate_ref[...] * (conv + z_ref[...] * skip_ref[...])
    seq = lax.broadcasted_iota(jnp.int32, z.shape, 0) * FFT_C + pl.program_id(0)
    o_ref[...] = jnp.where(seq < n_valid, z, 0.0)


def fft_stage3_gate(y, f1inv, z, zcol0, gates, gcol0, skip, n_valid, td=2048):
    _, ka, cols = y.shape
    D = cols // FFT_C
    P = z.shape[0]
    R = P // FFT_C
    td = _tile(D, td)
    nblk = D // td
    zs, gs = z.shape[1] // td, gates.shape[1] // td
    zc, gc = zcol0 // td, gcol0 // td
    out = pl.pallas_call(
        functools.partial(_fft_stage3_body, n_valid=n_valid),
        grid=(FFT_C, nblk),
        in_specs=[pl.BlockSpec((2, ka, td), lambda s, j: (0, 0, s * nblk + j)),
                  pl.BlockSpec((R, 2 * ka), lambda s, j: (0, 0)),
                  pl.BlockSpec((R, td), lambda s, j: (0, s * zs + zc + j)),
                  pl.BlockSpec((R, td), lambda s, j: (0, s * gs + gc + j)),
                  pl.BlockSpec((1, td), lambda s, j: (0, j))],
        out_specs=pl.BlockSpec((R, td), lambda s, j: (0, s * nblk + j)),
        out_shape=jax.ShapeDtypeStruct((R, FFT_C * D), F32),
        compiler_params=_cparams("parallel", "parallel"),
        name="fft_stage3_gate",
    )(y, f1inv, z.reshape(R, FFT_C * z.shape[1]), gates.reshape(R, FFT_C * gates.shape[1]), skip)
    return out.reshape(P, D)


def _dft_tables(R):
    C = FFT_C
    A = 2 * R
    N = A * C
    ka_pad = 2 * FFT_KB
    KA = -(-(R + 1) // ka_pad) * ka_pad
    ka = jnp.arange(KA, dtype=jnp.int32)
    ang1 = (2.0 * math.pi / A) * ((ka[:, None] * jnp.arange(R, dtype=jnp.int32)[None, :]) % A).astype(F32)
    used = (ka <= R).astype(F32)[:, None]
    mirror = jnp.where((ka == 0) | (ka == R), 1.0, 2.0)[:, None] * used
    c1, s1 = jnp.cos(ang1), jnp.sin(ang1)
    f1 = jnp.concatenate([c1 * used, -s1 * used], axis=0).astype(BF16)
    f1inv = jnp.concatenate([(c1 * mirror).T, (-s1 * mirror).T], axis=1).astype(BF16)
    n1 = jnp.arange(C, dtype=jnp.int32)
    k = ka[:, None, None] + A * n1[None, :, None]
    ang2 = (2.0 * math.pi / N) * ((k * n1[None, None, :]) % N).astype(F32)
    er, ei = jnp.cos(ang2), -jnp.sin(ang2)
    ert, eit = er.transpose(0, 2, 1), ei.transpose(0, 2, 1)
    mf = jnp.concatenate([jnp.concatenate([er, -ei], axis=2), jnp.concatenate([ei, er], axis=2)], axis=1)
    mi = jnp.concatenate([jnp.concatenate([ert, eit], axis=2), jnp.concatenate([-eit, ert], axis=2)], axis=1)
    return f1, f1inv, mf.astype(BF16), mi.astype(BF16), N


def _pos_features(L, P):
    bands = (POS_EMB_DIM - 1) // 2
    i = jnp.arange(P, dtype=F32)
    t = jnp.linspace(0.0, 1.0, L, dtype=F32)
    t = jnp.concatenate([t, jnp.ones((P - L,), F32)])[:, None]
    w = (2.0 * math.pi / L) * i[:, None]
    f = jnp.linspace(1e-4, bands - 1, bands, dtype=F32)[None, :]
    z = jnp.concatenate([t, jnp.cos(f * w), -jnp.sin(f * w)], axis=-1)
    return jnp.pad(z, ((0, 0), (0, LANE - POS_EMB_DIM))), t


def hyena_group(uc, L, D, filt, skip):
    P = uc.shape[0]
    R = P // FFT_C
    w1p, b1, w2, b2, w3f, freq, dec = filt
    f1, f1inv, mf, mi, N = _dft_tables(R)
    zfeat, tcol = _pos_features(L, P)
    hw, ssq = hyena_filter(zfeat, tcol, w1p, b1, w2, b2, w3f, freq, dec, L)
    n_orders = skip.shape[0]
    ssq = jnp.sum(ssq[::SUBLANE], axis=0).reshape(n_orders, 2, D)
    h0 = hw[0].reshape(n_orders, 2, D)
    total = ssq[:, 0] + ssq[:, 1] + 2.0 * h0[:, 0] * h0[:, 1]
    scale = lax.rsqrt(total + EPS) / N
    yh = fft_stage1(hw, 0, hw.shape[1], f1)
    z = uc
    for n in range(n_orders):
        y = fft_stage1(z, 0, D, f1)
        y = fft_stage2(y, yh, n, scale[n:n + 1], mf, mi)
        z = fft_stage3_gate(y, f1inv, z, 0, uc, (n + 1) * D, skip[n:n + 1], L)
    return z


def _rope(x, rot_ref, cos_ref, sin_ref):
    xr = jnp.dot(x.astype(BF16), rot_ref[...], preferred_element_type=F32)
    return x * cos_ref[...] + xr * sin_ref[...]


def _q_head_body(c_ref, gq_ref, w_ref, gh_ref, rot_ref, cos_ref, sin_ref, q_ref, cn_ref, *, q_lora, scale):
    @pl.when(pl.program_id(1) == 0)
    def _():
        cn_ref[...] = _rms(c_ref[:, :q_lora], gq_ref[...]).astype(BF16)

    q = jnp.dot(cn_ref[...], w_ref[0], preferred_element_type=F32)
    q = _rope(_rms(q, gh_ref[...]), rot_ref, cos_ref, sin_ref)
    q_ref[0] = (q * scale).astype(q_ref.dtype)


def mla_q_heads(cqkv, g_q, w_uq_h, g_qhead, rot, cosf, sinf, q_lora, tm=TOK_TILE):
    T, CW = cqkv.shape
    H = w_uq_h.shape[0]
    return pl.pallas_call(
        functools.partial(_q_head_body, q_lora=q_lora, scale=QK_HEAD ** -0.5 * math.log2(math.e)),
        grid=(T // tm, H),
        in_specs=[pl.BlockSpec((tm, CW), lambda i, h: (i, 0)),
                  pl.BlockSpec((1, q_lora), lambda i, h: (0, 0)),
                  pl.BlockSpec((1, q_lora, QK_HEAD), lambda i, h: (h, 0, 0)),
                  pl.BlockSpec((1, QK_HEAD), lambda i, h: (0, 0)),
                  pl.BlockSpec((QK_HEAD, QK_HEAD), lambda i, h: (0, 0)),
                  pl.BlockSpec((tm, QK_HEAD), lambda i, h: (i, 0)),
                  pl.BlockSpec((tm, QK_HEAD), lambda i, h: (i, 0))],
        out_specs=pl.BlockSpec((1, tm, QK_HEAD), lambda i, h: (h, i, 0)),
        out_shape=jax.ShapeDtypeStruct((H, T, QK_HEAD), BF16),
        scratch_shapes=[pltpu.VMEM((tm, q_lora), BF16)],
        compiler_params=_cparams("parallel", "arbitrary"),
        name="mla_q_heads",
    )(cqkv, g_q.reshape(1, -1), w_uq_h, g_qhead.reshape(1, -1), rot, cosf, sinf)


def _kv_head_body(c_ref, gkv_ref, wk_ref, wvt_ref, gh_ref, rot_ref, cos_ref, sin_ref, k_ref, vt_ref, cn_ref,
                  *, q_lora, kv_lora):
    @pl.when(pl.program_id(1) == 0)
    def _():
        cn_ref[...] = _rms(c_ref[:, q_lora:q_lora + kv_lora], gkv_ref[...]).astype(BF16)

    cn = cn_ref[...]
    k_nope = jnp.dot(cn, wk_ref[0], preferred_element_type=F32)
    k_pe = c_ref[:, q_lora + kv_lora:q_lora + kv_lora + QK_ROPE]
    k = jnp.concatenate([k_nope, k_pe], axis=-1)
    k = _rope(_rms(k, gh_ref[...]), rot_ref, cos_ref, sin_ref)
    k_ref[0] = k.astype(k_ref.dtype)
    vt = lax.dot_general(wvt_ref[0], cn, (((1,), (1,)), ((), ())), preferred_element_type=F32)
    vt_ref[0, :V_HEAD, :] = vt.astype(vt_ref.dtype)
    extra = vt_ref.shape[1] - V_HEAD
    row = lax.broadcasted_iota(jnp.int32, (extra, vt.shape[1]), 0)
    vt_ref[0, V_HEAD:, :] = jnp.where(row == 0, 1.0, 0.0).astype(vt_ref.dtype)


def mla_kv_heads(cqkv, g_kv, w_uk_h, w_uvt_h, g_khead, rot, cosf, sinf, q_lora, kv_lora, tm=TOK_TILE):
    T, CW = cqkv.shape
    H = w_uk_h.shape[0]
    return pl.pallas_call(
        functools.partial(_kv_head_body, q_lora=q_lora, kv_lora=kv_lora),
        grid=(T // tm, H),
        in_specs=[pl.BlockSpec((tm, CW), lambda i, h: (i, 0)),
                  pl.BlockSpec((1, kv_lora), lambda i, h: (0, 0)),
                  pl.BlockSpec((1, kv_lora, QK_NOPE), lambda i, h: (h, 0, 0)),
                  pl.BlockSpec((1, V_HEAD, kv_lora), lambda i, h: (h, 0, 0)),
                  pl.BlockSpec((1, QK_HEAD), lambda i, h: (0, 0)),
                  pl.BlockSpec((QK_HEAD, QK_HEAD), lambda i, h: (0, 0)),
                  pl.BlockSpec((tm, QK_HEAD), lambda i, h: (i, 0)),
                  pl.BlockSpec((tm, QK_HEAD), lambda i, h: (i, 0))],
        out_specs=[pl.BlockSpec((1, tm, QK_HEAD), lambda i, h: (h, i, 0)),
                   pl.BlockSpec((1, VT_ROWS, tm), lambda i, h: (h, 0, i))],
        out_shape=[jax.ShapeDtypeStruct((H, T, QK_HEAD), BF16),
                   jax.ShapeDtypeStruct((H, VT_ROWS, T), BF16)],
        scratch_shapes=[pltpu.VMEM((tm, kv_lora), BF16)],
        compiler_params=_cparams("parallel", "arbitrary"),
        name="mla_kv_heads",
    )(cqkv, g_kv.reshape(1, -1), w_uk_h, w_uvt_h, g_khead.reshape(1, -1), rot, cosf, sinf)


def _flash_body(q_ref, k_hbm, vt_hbm, o_prev_ref, o_ref, k_buf, vt_buf, acc_ref, s_buf, p_buf, sem,
                *, row0, n_chunks, last_valid, tk):
    del o_prev_ref
    h = pl.program_id(0)
    n_keys = k_buf.shape[0]

    @pl.when(pl.program_id(1) == 0)
    def _():
        ck = pltpu.make_async_copy(k_hbm.at[h, pl.ds(row0, n_keys)], k_buf, sem.at[0])
        cv = pltpu.make_async_copy(vt_hbm.at[h, :, pl.ds(row0, n_keys)], vt_buf, sem.at[1])
        ck.start()
        cv.start()
        ck.wait()
        cv.wait()

    q = q_ref[0]
    tq = q.shape[0]
    acc_ref[...] = jnp.zeros_like(acc_ref)

    def key_start(c):
        return c * tk if isinstance(c, int) else pl.multiple_of(c * tk, tk)

    def qk(c, slot):
        s_buf[slot] = lax.dot_general(k_buf[pl.ds(key_start(c), tk), :], q, (((1,), (1,)), ((), ())),
                                      preferred_element_type=F32)

    def softmax(slot, m, n_valid=None):
        s = s_buf[slot]
        if n_valid is not None:
            s = jnp.where(lax.broadcasted_iota(jnp.int32, s.shape, 0) < n_valid, s, NEG_BIG)
        m_new = jnp.maximum(m, jnp.max(s, axis=0, keepdims=True))
        p_buf[slot] = jnp.exp2(s - m_new).astype(BF16)
        return m_new, jnp.exp2(m - m_new)

    def pv(c, slot, alpha):
        acc_ref[...] = alpha * acc_ref[...] + jnp.dot(vt_buf[:, pl.ds(key_start(c), tk)], p_buf[slot],
                                                      preferred_element_type=F32)

    def step(c, par, m, alpha):
        pv(c, par, alpha)
        m, alpha = softmax(1 - par, m)
        qk(c + 2, par)
        return m, alpha

    def pair(i, carry):
        m, alpha = step(2 * i, 0, *carry)
        return step(2 * i + 1, 1, m, alpha)

    n = n_chunks
    qk(0, 0)
    qk(1, 1)
    m, alpha = softmax(0, jnp.full((1, tq), NEG_BIG, F32))
    m, alpha = lax.fori_loop(0, (n - 2) // 2, pair, (m, alpha))
    if (n - 2) % 2:
        m, alpha = step(n - 3, 0, m, alpha)
    par = (n - 2) % 2
    pv(n - 2, par, alpha)
    m, alpha = softmax(1 - par, m, last_valid)
    pv(n - 1, 1 - par, alpha)
    acc = acc_ref[...]
    o = acc[:V_HEAD] / acc[V_HEAD:V_HEAD + 1]
    o_ref[...] = o.T.astype(o_ref.dtype)


def flash_attention(q, k, vt, o_prev, row0, n_rows, n_valid, tq=512, tk=512):
    H, T, _ = q.shape
    q0 = row0 // tq
    n_chunks = -(-n_valid // tk)
    last_valid = n_valid - (n_chunks - 1) * tk
    assert n_chunks >= 3 and n_chunks * tk <= n_rows
    return pl.pallas_call(
        functools.partial(_flash_body, row0=row0, n_chunks=n_chunks, last_valid=last_valid, tk=tk),
        grid=(H, n_rows // tq),
        in_specs=[pl.BlockSpec((1, tq, QK_HEAD), lambda h, i: (h, q0 + i, 0)),
                  pl.BlockSpec(memory_space=pl.ANY),
                  pl.BlockSpec(memory_space=pl.ANY),
                  pl.BlockSpec(memory_space=pl.ANY)],
        out_specs=pl.BlockSpec((tq, V_HEAD), lambda h, i: (q0 + i, h)),
        out_shape=jax.ShapeDtypeStruct((T, H * V_HEAD), BF16),
        scratch_shapes=[pltpu.VMEM((n_rows, QK_HEAD), BF16), pltpu.VMEM((VT_ROWS, n_rows), BF16),
                        pltpu.VMEM((VT_ROWS, tq), F32), pltpu.VMEM((2, tk, tq), F32), pltpu.VMEM((2, tk, tq), BF16),
                        pltpu.SemaphoreType.DMA((2,))],
        input_output_aliases={3: 0},
        compiler_params=_cparams("arbitrary", "arbitrary"),
        name="flash_attention",
    )(q, k, vt, o_prev)


META_E1, META_E2, META_G1, META_G2, META_R1, META_R2 = range(6)


def _router_body(h_ref, g_ref, wr_ref, xn_ref, meta_ref, cnt_ref, carry_ref, *, token_ranges):
    i = pl.program_id(0)

    @pl.when(i == 0)
    def _():
        carry_ref[...] = jnp.zeros_like(carry_ref)

    xn = _rms(h_ref[...], g_ref[...])
    xn_ref[...] = xn
    logits = jnp.dot(xn, wr_ref[...], precision=lax.Precision.HIGHEST, preferred_element_type=F32)
    tb = logits.shape[0]
    lane = lax.broadcasted_iota(jnp.int32, logits.shape, 1)
    lg = jnp.where(lane < N_EXPERTS, logits, NEG_BIG)
    m1 = jnp.max(lg, axis=-1, keepdims=True)
    i1 = jnp.min(jnp.where(lg == m1, lane, LANE), axis=-1, keepdims=True)
    lg2 = jnp.where(lane == i1, NEG_BIG, lg)
    m2 = jnp.max(lg2, axis=-1, keepdims=True)
    i2 = jnp.min(jnp.where(lg2 == m2, lane, LANE), axis=-1, keepdims=True)
    e = jnp.exp(m2 - m1)
    row = lax.broadcasted_iota(jnp.int32, (tb, 1), 0) + i * tb
    valid = functools.reduce(jnp.logical_or, [(row >= a) & (row < b) for a, b in token_ranges])
    g1 = jnp.where(valid, 1.0 / (1.0 + e), 0.0)
    g2 = jnp.where(valid, e / (1.0 + e), 0.0)
    oh = jnp.where(valid & ((lane == i1) | (lane == i2)), 1.0, 0.0)
    r = lax.broadcasted_iota(jnp.int32, (tb, tb), 0)
    c = lax.broadcasted_iota(jnp.int32, (tb, tb), 1)
    tri = jnp.where(c < r, 1.0, 0.0).astype(BF16)
    pre = jnp.dot(tri, oh.astype(BF16), preferred_element_type=F32) + carry_ref[0:1, :]
    r1 = jnp.sum(jnp.where(lane == i1, pre, 0.0), axis=-1, keepdims=True)
    r2 = jnp.sum(jnp.where(lane == i2, pre, 0.0), axis=-1, keepdims=True)
    carry_ref[...] += jnp.broadcast_to(jnp.sum(oh, axis=0, keepdims=True), carry_ref.shape)
    meta = jnp.zeros(logits.shape, F32)
    for idx, val in ((META_E1, i1.astype(F32)), (META_E2, i2.astype(F32)), (META_G1, g1), (META_G2, g2),
                     (META_R1, r1), (META_R2, r2)):
        meta = jnp.where(lane == idx, val, meta)
    meta_ref[...] = meta
    cnt_ref[...] = carry_ref[...]


def moe_router(h, g, w_router_pad, token_ranges, tb=256):
    T, D = h.shape
    return pl.pallas_call(
        functools.partial(_router_body, token_ranges=token_ranges),
        grid=(T // tb,),
        in_specs=[pl.BlockSpec((tb, D), lambda i: (i, 0)),
                  pl.BlockSpec((1, D), lambda i: (0, 0)),
                  pl.BlockSpec((D, LANE), lambda i: (0, 0))],
        out_specs=[pl.BlockSpec((tb, D), lambda i: (i, 0)),
                   pl.BlockSpec((tb, LANE), lambda i: (i, 0)),
                   pl.BlockSpec((SUBLANE, LANE), lambda i: (0, 0))],
        out_shape=[jax.ShapeDtypeStruct((T, D), F32),
                   jax.ShapeDtypeStruct((T, LANE), F32),
                   jax.ShapeDtypeStruct((SUBLANE, LANE), F32)],
        scratch_shapes=[pltpu.VMEM((SUBLANE, LANE), F32)],
        compiler_params=_cparams("arbitrary"),
        name="moe_router",
    )(h, g.reshape(1, D), w_router_pad)


def _expert_up_body(te_ref, nu_ref, src_ref, xn_hbm, wg_ref, wu_ref, o_ref, xf_ref, xb_ref, sem):
    active = pl.program_id(0) < nu_ref[0]
    tm = xf_ref.shape[0]

    @pl.when(active & (pl.program_id(1) == 0))
    def _():
        def row_copy(r):
            return pltpu.make_async_copy(xn_hbm.at[pl.ds(src_ref[0, 0, r], 1)], xf_ref.at[pl.ds(r, 1)], sem)

        def start(r, c):
            row_copy(r).start()
            return c

        def wait(r, c):
            row_copy(r).wait()
            return c

        lax.fori_loop(0, tm, start, 0, unroll=DMA_UNROLL)
        lax.fori_loop(0, tm, wait, 0, unroll=DMA_UNROLL)
        xb_ref[...] = xf_ref[...].astype(BF16)

    @pl.when(active)
    def _():
        x = xb_ref[...]
        a = jnp.dot(x, wg_ref[0], preferred_element_type=F32)
        b = jnp.dot(x, wu_ref[0], preferred_element_type=F32)
        o_ref[...] = _silu_mul(a, b).astype(o_ref.dtype)

    @pl.when(jnp.logical_not(active))
    def _():
        o_ref[...] = jnp.zeros_like(o_ref)


def expert_up(xn, src, wg, wu, tile_expert, n_used, tm, tf=1024):
    D = xn.shape[1]
    n_tiles = src.shape[0] // tm
    F = wg.shape[2]
    tf = _tile(F, tf)
    grid_spec = pltpu.PrefetchScalarGridSpec(
        num_scalar_prefetch=2,
        grid=(n_tiles, F // tf),
        in_specs=[pl.BlockSpec((1, 1, tm), lambda i, j, te, nu: (i, 0, 0), memory_space=pltpu.SMEM),
                  pl.BlockSpec(memory_space=pl.ANY),
                  pl.BlockSpec((1, D, tf), lambda i, j, te, nu: (te[i], 0, j)),
                  pl.BlockSpec((1, D, tf), lambda i, j, te, nu: (te[i], 0, j))],
        out_specs=pl.BlockSpec((tm, tf), lambda i, j, te, nu: (i, j)),
        scratch_shapes=[pltpu.VMEM((tm, D), F32), pltpu.VMEM((tm, D), BF16), pltpu.SemaphoreType.DMA(())],
    )
    return pl.pallas_call(
        _expert_up_body,
        grid_spec=grid_spec,
        out_shape=jax.ShapeDtypeStruct((n_tiles * tm, F), BF16),
        compiler_params=_cparams("arbitrary", "arbitrary"),
        name="moe_expert_up",
    )(tile_expert, n_used, src.reshape(n_tiles, 1, tm), xn, wg, wu)


def _expert_down_body(te_ref, nu_ref, a_ref, w_ref, o_ref):
    @pl.when(pl.program_id(0) < nu_ref[0])
    def _():
        o_ref[...] = jnp.dot(a_ref[...], w_ref[0], preferred_element_type=F32)

    @pl.when(pl.program_id(0) >= nu_ref[0])
    def _():
        o_ref[...] = jnp.zeros_like(o_ref)


def expert_down(act, wd, tile_expert, n_used, tm, tn=512):
    S, F = act.shape
    D = wd.shape[2]
    tn = _tile(D, tn)
    grid_spec = pltpu.PrefetchScalarGridSpec(
        num_scalar_prefetch=2,
        grid=(S // tm, D // tn),
        in_specs=[pl.BlockSpec((tm, F), lambda i, j, te, nu: (i, 0)),
                  pl.BlockSpec((1, F, tn), lambda i, j, te, nu: (te[i], 0, j))],
        out_specs=pl.BlockSpec((tm, tn), lambda i, j, te, nu: (i, j)),
    )
    return pl.pallas_call(
        _expert_down_body,
        grid_spec=grid_spec,
        out_shape=jax.ShapeDtypeStruct((S, D), F32),
        compiler_params=_cparams("parallel", "parallel"),
        name="moe_expert_down",
    )(tile_expert, n_used, act, wd)


def _combine_body(pos_ref, h_ref, meta_ref, ys_hbm, o_ref, buf, sem, *, tb):
    def row_copy(r, k):
        return pltpu.make_async_copy(ys_hbm.at[pl.ds(pos_ref[0, 0, 2 * r + k], 1)],
                                     buf.at[k, pl.ds(r, 1)], sem)

    def start(r, c):
        row_copy(r, 0).start()
        row_copy(r, 1).start()
        return c

    def wait(r, c):
        row_copy(r, 0).wait()
        row_copy(r, 1).wait()
        return c

    lax.fori_loop(0, tb, start, 0, unroll=DMA_UNROLL)
    lax.fori_loop(0, tb, wait, 0, unroll=DMA_UNROLL)
    g1 = meta_ref[:, META_G1:META_G1 + 1]
    g2 = meta_ref[:, META_G2:META_G2 + 1]
    o_ref[...] = h_ref[...] + g1 * buf[0] + g2 * buf[1]


def moe_combine(h, meta, pos, ys, tb=256):
    T, D = h.shape
    pos3 = pos.reshape(T // tb, 1, 2 * tb)
    return pl.pallas_call(
        functools.partial(_combine_body, tb=tb),
        grid=(T // tb,),
        in_specs=[pl.BlockSpec((1, 1, 2 * tb), lambda i: (i, 0, 0), memory_space=pltpu.SMEM),
                  pl.BlockSpec((tb, D), lambda i: (i, 0)),
                  pl.BlockSpec((tb, LANE), lambda i: (i, 0)),
                  pl.BlockSpec(memory_space=pl.ANY)],
        out_specs=pl.BlockSpec((tb, D), lambda i: (i, 0)),
        out_shape=jax.ShapeDtypeStruct((T, D), F32),
        scratch_shapes=[pltpu.VMEM((2, tb, D), F32), pltpu.SemaphoreType.DMA(())],
        compiler_params=_cparams("arbitrary"),
        name="moe_combine",
    )(pos3, h, meta, ys)


def moe_layer(h, g, w_router, wg, wu, wd, token_ranges, tm=512):
    T, D = h.shape
    wr = jnp.pad(w_router, ((0, 0), (0, LANE - N_EXPERTS)))
    xn, meta, cnt = moe_router(h, g, wr, token_ranges)
    counts = cnt[0, :N_EXPERTS].astype(jnp.int32)
    tiles = (counts + tm - 1) // tm
    tile_end = jnp.cumsum(tiles)
    starts = (tile_end - tiles) * tm
    n_tok = sum(b - a for a, b in token_ranges)
    n_tiles = (2 * n_tok + N_EXPERTS * (tm - 1)) // tm
    e_idx = meta[:, META_E1:META_E2 + 1].astype(jnp.int32)
    rank = meta[:, META_R1:META_R2 + 1].astype(jnp.int32)
    row = jnp.arange(T, dtype=jnp.int32)[:, None]
    valid = functools.reduce(jnp.logical_or, [(row >= a) & (row < b) for a, b in token_ranges])
    pos = starts[e_idx] + rank
    tile_expert = jnp.minimum(jnp.sum(jnp.arange(n_tiles)[:, None] >= tile_end[None, :], axis=1),
                              N_EXPERTS - 1).astype(jnp.int32)
    n_used = tile_end[-1:].astype(jnp.int32)
    tok = jnp.repeat(row[:, 0], 2)
    src = jnp.zeros((n_tiles * tm,), jnp.int32).at[jnp.where(valid, pos, n_tiles * tm).reshape(-1)].set(
        tok, mode="drop")
    act = expert_up(xn, src, wg, wu, tile_expert, n_used, tm)
    ys = expert_down(act, wd, tile_expert, n_used, tm)
    return moe_combine(h, meta, jnp.where(valid, pos, 0), ys)


def _padded_len(L):
    r = -(-L // FFT_C)
    r_align = TOK_TILE // FFT_C
    r = -(-r // r_align) * r_align
    return r * FFT_C


def _rope_tables(lens, pads):
    half = QK_ROPE // 2
    inv = 1.0 / (ROPE_BASE ** (jnp.arange(0, QK_ROPE, 2, dtype=F32) / QK_ROPE))
    cs, ss = [], []
    for L, P in zip(lens, pads):
        ang = jnp.arange(P, dtype=F32)[:, None] * inv[None, :]
        c, s = jnp.cos(ang), jnp.sin(ang)
        cs.append(jnp.concatenate([jnp.ones((P, QK_NOPE), F32), c, c], axis=-1))
        ss.append(jnp.concatenate([jnp.zeros((P, QK_NOPE), F32), s, s], axis=-1))
    rot = np.zeros((QK_HEAD, QK_HEAD), np.float32)
    for c in range(QK_NOPE, QK_NOPE + half):
        rot[c + half, c] = -1.0
        rot[c, c + half] = 1.0
    return jnp.concatenate(cs), jnp.concatenate(ss), jnp.asarray(rot, BF16)


def kernel(x_prompt, x_sample, meta_tokens, norm_mix, norm_ffn, hy_w_in, hy_conv_w, hy_conv_b, hy_filt_w1, hy_filt_b1, hy_filt_w2, hy_filt_b2, hy_filt_w3, hy_filt_freq, hy_filt_log_decay, hy_skip, hy_w_out, ffn_w_gate, ffn_w_up, ffn_w_down, mla_w_dq, mla_g_q, mla_w_uq, mla_w_dkv, mla_g_kv, mla_w_ukv, mla_g_qhead, mla_g_khead, mla_w_o, moe_w_router, moe_w_gate, moe_w_up, moe_w_down):
    D = x_prompt.shape[-1]
    n_meta = meta_tokens.shape[0]
    assert x_prompt.shape[0] == 1 and x_sample.shape[0] == 1
    groups = (x_prompt[0], x_sample[0])
    lens = [n_meta + g.shape[0] for g in groups]
    pads = [_padded_len(L) for L in lens]
    row0s = [0, pads[0]]
    T = sum(pads)
    assert T % TOK_TILE == 0 and all(p % TOK_TILE == 0 for p in pads)

    parts = []
    for g, L, P in zip(groups, lens, pads):
        parts += [meta_tokens.astype(F32), g, jnp.zeros((P - L, D), F32)]
    h = jnp.concatenate(parts, axis=0)

    bf = lambda w: w.astype(BF16)

    u = norm_mm(h, norm_mix[0], [bf(hy_w_in)], lambda y: y, BF16, 1024, "hyena_in_proj")
    n_orders = hy_skip.shape[0]
    fh = hy_filt_w2.shape[0]
    filt = (jnp.pad(hy_filt_w1, ((0, LANE - POS_EMB_DIM), (0, 0))), hy_filt_b1, hy_filt_w2, hy_filt_b2,
            bf(hy_filt_w3.reshape(fh, n_orders * 2 * D)), hy_filt_freq, hy_filt_log_decay.reshape(-1))
    zs = []
    for L, P, r0 in zip(lens, pads, row0s):
        uc = short_conv(u, hy_conv_w, hy_conv_b, r0, P, L)
        zs.append(hyena_group(uc, L, D, filt, hy_skip))
    z = jnp.concatenate(zs, axis=0)
    h = mm_res(z, bf(hy_w_out), h, 1024, "hyena_out_proj")

    a = norm_mm(h, norm_ffn[0], [bf(ffn_w_gate), bf(ffn_w_up)], _silu_mul, BF16, 1024, "ffn_swiglu")
    h = mm_res(a, bf(ffn_w_down), h, 512, "ffn_down")

    q_lora = mla_w_dq.shape[1]
    kv_lora = mla_w_ukv.shape[0]
    cw = q_lora + mla_w_dkv.shape[1]
    cw_pad = -(-cw // LANE) * LANE
    w_down = jnp.pad(jnp.concatenate([mla_w_dq, mla_w_dkv], axis=1), ((0, 0), (0, cw_pad - cw)))
    cqkv = norm_mm(h, norm_mix[1], [bf(w_down)], lambda y: y, F32, cw_pad, "mla_down_proj")
    cosf, sinf, rot = _rope_tables(lens, pads)
    w_uq_h = bf(mla_w_uq.reshape(q_lora, MLA_HEADS, QK_HEAD).transpose(1, 0, 2))
    w_ukv_h = mla_w_ukv.reshape(kv_lora, MLA_HEADS, QK_NOPE + V_HEAD).transpose(1, 0, 2)
    w_uk_h = bf(w_ukv_h[:, :, :QK_NOPE])
    w_uvt_h = bf(w_ukv_h[:, :, QK_NOPE:].transpose(0, 2, 1))
    q = mla_q_heads(cqkv, mla_g_q, w_uq_h, mla_g_qhead, rot, cosf, sinf, q_lora)
    k, vt = mla_kv_heads(cqkv, mla_g_kv, w_uk_h, w_uvt_h, mla_g_khead, rot, cosf, sinf, q_lora, kv_lora)
    o = jnp.zeros((T, MLA_HEADS * V_HEAD), BF16)
    for L, P, r0 in zip(lens, pads, row0s):
        o = flash_attention(q, k, vt, o, r0, P, L)
    h = mm_res(o, bf(mla_w_o), h, 1024, "mla_out_proj")

    token_ranges = tuple((r0, r0 + L) for r0, L in zip(row0s, lens))
    h = moe_layer(h, norm_ffn[1], moe_w_router, bf(moe_w_gate), bf(moe_w_up), bf(moe_w_down), token_ranges)

    outs = []
    for g, r0 in zip(groups, row0s):
        outs.append(h[r0 + n_meta:r0 + n_meta + g.shape[0]][None])
    return tuple(outs)
```

```python
import functools
import math

import numpy as np
import jax
import jax.numpy as jnp
from jax import lax
from jax.experimental import pallas as pl
from jax.experimental.pallas import tpu as pltpu

F32 = jnp.float32
BF16 = jnp.bfloat16

EPS = 1e-6
MLA_HEADS = 16
QK_NOPE = 128
QK_ROPE = 64
QK_HEAD = QK_NOPE + QK_ROPE
V_HEAD = 128
VT_ROWS = V_HEAD + 16
ROPE_BASE = 10000.0
N_EXPERTS = 8
POS_EMB_DIM = 33

LANE = 128
SUBLANE = 8
VMEM_LIMIT_BYTES = 56 * 1024 * 1024

FFT_C = 128
FFT_JB = 2 * SUBLANE
FFT_KB = SUBLANE

TOK_TILE = 512
DMA_UNROLL = 8
NEG_BIG = -1e30


def _cparams(*sem):
    return pltpu.CompilerParams(dimension_semantics=sem, vmem_limit_bytes=VMEM_LIMIT_BYTES)


def _tile(n, pref):
    if n <= pref:
        return n
    t = pref - pref % LANE
    while n % t:
        t -= LANE
    return t


def _rms(x, g):
    ms = jnp.mean(x * x, axis=-1, keepdims=True)
    return x * lax.rsqrt(ms + EPS) * g


def _norm_mm_body(x_ref, g_ref, *rest, n_w, epilogue):
    w_refs, o_ref, xn_ref = rest[:n_w], rest[n_w], rest[n_w + 1]

    @pl.when(pl.program_id(1) == 0)
    def _():
        xn_ref[...] = _rms(x_ref[...], g_ref[...]).astype(BF16)

    xn = xn_ref[...]
    ys = [jnp.dot(xn, w[...], preferred_element_type=F32) for w in w_refs]
    o_ref[...] = epilogue(*ys).astype(o_ref.dtype)


def norm_mm(x, g, ws, epilogue, out_dtype, tn, name, tm=TOK_TILE):
    T, K = x.shape
    N = ws[0].shape[1]
    tn = _tile(N, tn)
    n_w = len(ws)
    return pl.pallas_call(
        functools.partial(_norm_mm_body, n_w=n_w, epilogue=epilogue),
        grid=(T // tm, N // tn),
        in_specs=[pl.BlockSpec((tm, K), lambda i, j: (i, 0)),
                  pl.BlockSpec((1, K), lambda i, j: (0, 0))]
        + [pl.BlockSpec((K, tn), lambda i, j: (0, j))] * n_w,
        out_specs=pl.BlockSpec((tm, tn), lambda i, j: (i, j)),
        out_shape=jax.ShapeDtypeStruct((T, N), out_dtype),
        scratch_shapes=[pltpu.VMEM((tm, K), BF16)],
        compiler_params=_cparams("parallel", "arbitrary"),
        name=name,
    )(x, g.reshape(1, K), *ws)


def _mm_res_body(x_ref, w_ref, r_ref, o_ref):
    o_ref[...] = r_ref[...] + jnp.dot(x_ref[...].astype(BF16), w_ref[...], preferred_element_type=F32)


def mm_res(x, w, res, tn, name, tm=TOK_TILE):
    T, K = x.shape
    N = w.shape[1]
    tn = _tile(N, tn)
    return pl.pallas_call(
        _mm_res_body,
        grid=(T // tm, N // tn),
        in_specs=[pl.BlockSpec((tm, K), lambda i, j: (i, 0)),
                  pl.BlockSpec((K, tn), lambda i, j: (0, j)),
                  pl.BlockSpec((tm, tn), lambda i, j: (i, j))],
        out_specs=pl.BlockSpec((tm, tn), lambda i, j: (i, j)),
        out_shape=jax.ShapeDtypeStruct((T, N), F32),
        compiler_params=_cparams("parallel", "parallel"),
        name=name,
    )(x, w, res)


def _silu_mul(a, b):
    return a * (1.0 / (1.0 + jnp.exp(-a))) * b


def _short_conv_body(u_ref, prev_ref, next_ref, w_ref, b_ref, o_ref, *, row0, n_valid, n_rows_total, halo):
    i = pl.program_id(0)
    tm = u_ref.shape[0]
    u = u_ref[...].astype(F32)
    row = lax.broadcasted_iota(jnp.int32, u.shape, 0)
    g_row = row0 + i * tm
    prev_row = jnp.where(g_row == 0, 0.0, prev_ref[...].astype(F32)[halo - 1:halo, :])
    next_row = jnp.where(g_row + tm == n_rows_total, 0.0, next_ref[...].astype(F32)[0:1, :])
    up = jnp.where(row == 0, prev_row, pltpu.roll(u, 1, 0))
    dn = jnp.where(row == tm - 1, next_row, pltpu.roll(u, tm - 1, 0))
    y = b_ref[...] + up * w_ref[0:1, :] + u * w_ref[1:2, :] + dn * w_ref[2:3, :]
    o_ref[...] = jnp.where(i * tm + row < n_valid, y, 0.0)


def short_conv(u, w, b, row0, n_rows, n_valid, tm=TOK_TILE, tn=1024):
    T, C3 = u.shape
    tn = _tile(C3, tn)
    halo = 16
    r0 = row0 // tm
    hb = tm // halo
    nhalo = T // halo
    return pl.pallas_call(
        functools.partial(_short_conv_body, row0=row0, n_valid=n_valid, n_rows_total=T, halo=halo),
        grid=(n_rows // tm, C3 // tn),
        in_specs=[pl.BlockSpec((tm, tn), lambda i, j: (r0 + i, j)),
                  pl.BlockSpec((halo, tn), lambda i, j: (jnp.maximum((r0 + i) * hb - 1, 0), j)),
                  pl.BlockSpec((halo, tn), lambda i, j: (jnp.minimum((r0 + i + 1) * hb, nhalo - 1), j)),
                  pl.BlockSpec((3, tn), lambda i, j: (0, j)),
                  pl.BlockSpec((1, tn), lambda i, j: (0, j))],
        out_specs=pl.BlockSpec((tm, tn), lambda i, j: (i, j)),
        out_shape=jax.ShapeDtypeStruct((n_rows, C3), F32),
        compiler_params=_cparams("parallel", "parallel"),
        name="hyena_short_conv",
    )(u, u, u, w, b.reshape(1, C3))


def _filter_body(z_ref, w1_ref, b1_ref, w2_ref, b2_ref, w3_ref, fr_ref, dec_ref, t_ref,
                 h_ref, ssq_ref, hdn_ref, *, n_valid):
    i = pl.program_id(0)
    tm = z_ref.shape[0]

    @pl.when(pl.program_id(1) == 0)
    def _():
        hi = lax.Precision.HIGHEST
        a = jnp.dot(z_ref[...], w1_ref[...], precision=hi, preferred_element_type=F32) + b1_ref[...]
        a = jnp.sin(fr_ref[0:1, :] * a)
        a = jnp.dot(a, w2_ref[...], precision=hi, preferred_element_type=F32) + b2_ref[...]
        hdn_ref[...] = jnp.sin(fr_ref[1:2, :] * a).astype(BF16)

    h = jnp.dot(hdn_ref[...], w3_ref[...], preferred_element_type=F32)
    window = jnp.exp(-t_ref[...] * jnp.exp(dec_ref[...]))
    row = lax.broadcasted_iota(jnp.int32, h.shape, 0) + i * tm
    h = jnp.where(row < n_valid, h * window, 0.0)
    h_ref[...] = h
    ssq_ref[...] = jnp.broadcast_to(jnp.sum(h * h, axis=0, keepdims=True), ssq_ref.shape)


def hyena_filter(zfeat, tcol, w1, b1, w2, b2, w3, freq, log_decay, n_valid, tm=512, tc=2048):
    P = zfeat.shape[0]
    FH, NC = w3.shape
    tc = _tile(NC, tc)
    return pl.pallas_call(
        functools.partial(_filter_body, n_valid=n_valid),
        grid=(P // tm, NC // tc),
        in_specs=[pl.BlockSpec((tm, LANE), lambda i, c: (i, 0)),
                  pl.BlockSpec((LANE, FH), lambda i, c: (0, 0)),
                  pl.BlockSpec((1, FH), lambda i, c: (0, 0)),
                  pl.BlockSpec((FH, FH), lambda i, c: (0, 0)),
                  pl.BlockSpec((1, FH), lambda i, c: (0, 0)),
                  pl.BlockSpec((FH, tc), lambda i, c: (0, c)),
                  pl.BlockSpec((2, FH), lambda i, c: (0, 0)),
                  pl.BlockSpec((1, tc), lambda i, c: (0, c)),
                  pl.BlockSpec((tm, 1), lambda i, c: (i, 0))],
        out_specs=[pl.BlockSpec((tm, tc), lambda i, c: (i, c)),
                   pl.BlockSpec((SUBLANE, tc), lambda i, c: (i, c))],
        out_shape=[jax.ShapeDtypeStruct((P, NC), F32),
                   jax.ShapeDtypeStruct((P // tm * SUBLANE, NC), F32)],
        scratch_shapes=[pltpu.VMEM((tm, FH), BF16)],
        compiler_params=_cparams("parallel", "arbitrary"),
        name="hyena_filter",
    )(zfeat, w1, b1.reshape(1, FH), w2, b2.reshape(1, FH), w3, freq, log_decay.reshape(1, NC), tcol)


def _fft_fwd_body(x_ref, g_ref, y_ref):
    xt = jnp.swapaxes(x_ref[...], 0, 1).astype(BF16)
    ys = [jnp.dot(g_ref[j], xt[j], preferred_element_type=F32) for j in range(FFT_JB)]
    y = jnp.swapaxes(jnp.stack(ys), 0, 1)
    y_ref[...] = y.reshape(y_ref.shape).astype(y_ref.dtype)


def fft_fwd(x3, col0, ncols, gfwd, td=256):
    R, C, _ = x3.shape
    n1 = gfwd.shape[1] // 2
    td = _tile(ncols, td)
    c0 = col0 // td
    return pl.pallas_call(
        _fft_fwd_body,
        grid=(C // FFT_JB, ncols // td),
        in_specs=[pl.BlockSpec((R, FFT_JB, td), lambda i, j: (0, i, c0 + j)),
                  pl.BlockSpec((FFT_JB, 2 * n1, R), lambda i, j: (i, 0, 0))],
        out_specs=pl.BlockSpec((2, n1, FFT_JB, td), lambda i, j: (0, 0, i, j)),
        out_shape=jax.ShapeDtypeStruct((2, n1, C, ncols), BF16),
        compiler_params=_cparams("parallel", "parallel"),
        name="fft_fwd",
    )(x3, gfwd)


def _spectral_body(y_ref, ya_ref, yb_ref, s_ref, wf_ref, wi_ref, o_ref):
    c, td = y_ref.shape[2], y_ref.shape[3]
    s = s_ref[...]
    for kk in range(FFT_KB):
        x = jnp.dot(wf_ref[...], y_ref[:, kk].reshape(2 * c, td), preferred_element_type=F32)
        sa = jnp.dot(wf_ref[...], ya_ref[:, kk].reshape(2 * c, td), preferred_element_type=F32)
        sb = jnp.dot(wf_ref[...], yb_ref[:, kk].reshape(2 * c, td), preferred_element_type=F32)
        xr, xi = x[:c], x[c:]
        kr, ki = (sa[:c] + sb[:c]) * s, (sa[c:] - sb[c:]) * s
        z = jnp.concatenate([xr * kr - xi * ki, xr * ki + xi * kr], axis=0).astype(BF16)
        o = jnp.dot(wi_ref[...], z, preferred_element_type=F32)
        o_ref[:, kk] = o.reshape(2, c, td).astype(o_ref.dtype)


def spectral_mul(y, yh, order, scale, wf, wi, td=1024):
    _, n1, C, D = y.shape
    td = _tile(D, td)
    ca = (order * 2 * D) // td
    cb = (order * 2 * D + D) // td
    mat = pl.BlockSpec((2 * C, 2 * C), lambda a, j: (0, 0))
    return pl.pallas_call(
        _spectral_body,
        grid=(n1 // FFT_KB, D // td),
        in_specs=[pl.BlockSpec((2, FFT_KB, C, td), lambda a, j: (0, a, 0, j)),
                  pl.BlockSpec((2, FFT_KB, C, td), lambda a, j: (0, a, 0, ca + j)),
                  pl.BlockSpec((2, FFT_KB, C, td), lambda a, j: (0, a, 0, cb + j)),
                  pl.BlockSpec((1, td), lambda a, j: (0, j)),
                  mat, mat],
        out_specs=pl.BlockSpec((2, FFT_KB, C, td), lambda a, j: (0, a, 0, j)),
        out_shape=jax.ShapeDtypeStruct((2, n1, C, D), BF16),
        compiler_params=_cparams("parallel", "parallel"),
        name="spectral_mul",
    )(y, yh, yh, scale, wf, wi)


def _fft_inv_body(y_ref, gi_ref, z_ref, gate_ref, skip_ref, o_ref, *, n_valid):
    i = pl.program_id(0)
    n1c2, jb, td = 2 * y_ref.shape[1], y_ref.shape[2], y_ref.shape[3]
    yt = jnp.swapaxes(y_ref[...].reshape(n1c2, jb, td).astype(F32), 0, 1).astype(BF16)
    convs = [jnp.dot(gi_ref[j], yt[j], preferred_element_type=F32) for j in range(jb)]
    conv = jnp.swapaxes(jnp.stack(convs), 0, 1)
    z = gate_ref[...] * (conv + z_ref[...] * skip_ref[...])
    seq = (lax.broadcasted_iota(jnp.int32, z.shape, 0) * FFT_C
           + lax.broadcasted_iota(jnp.int32, z.shape, 1) + i * jb)
    o_ref[...] = jnp.where(seq < n_valid, z, 0.0)


def fft_inv_gate(y, ginv, z3, zcol0, gate3, gcol0, skip, n_valid, td=256):
    _, n1, C, D = y.shape
    R = ginv.shape[1]
    td = _tile(D, td)
    zc, gc = zcol0 // td, gcol0 // td
    return pl.pallas_call(
        functools.partial(_fft_inv_body, n_valid=n_valid),
        grid=(C // FFT_JB, D // td),
        in_specs=[pl.BlockSpec((2, n1, FFT_JB, td), lambda i, j: (0, 0, i, j)),
                  pl.BlockSpec((FFT_JB, R, 2 * n1), lambda i, j: (i, 0, 0)),
                  pl.BlockSpec((R, FFT_JB, td), lambda i, j: (0, i, zc + j)),
                  pl.BlockSpec((R, FFT_JB, td), lambda i, j: (0, i, gc + j)),
                  pl.BlockSpec((1, 1, td), lambda i, j: (0, 0, j))],
        out_specs=pl.BlockSpec((R, FFT_JB, td), lambda i, j: (0, i, j)),
        out_shape=jax.ShapeDtypeStruct((R, C, D), F32),
        compiler_params=_cparams("parallel", "parallel"),
        name="fft_inv_gate",
    )(y, ginv, z3, gate3, skip.reshape(1, 1, -1))


def _dft_tables(R):
    C = FFT_C
    N = 2 * R * C
    n1_pad = 2 * FFT_KB
    n1c = -(-(R + 1) // n1_pad) * n1_pad
    k1 = jnp.arange(n1c, dtype=jnp.int32)[None, :, None]
    n1 = jnp.arange(R, dtype=jnp.int32)[None, None, :]
    n2 = jnp.arange(C, dtype=jnp.int32)[:, None, None]
    ang = (2.0 * math.pi / N) * ((k1 * (C * n1 + n2)) % N).astype(F32)
    used = (k1 <= R).astype(F32)
    mirror = jnp.where((k1 == 0) | (k1 == R), 1.0, 2.0) * used
    gr, gim = jnp.cos(ang), -jnp.sin(ang)
    gfwd = jnp.concatenate([gr * used, gim * used], axis=1).astype(BF16)
    ginv = jnp.concatenate([gr * mirror, gim * mirror], axis=1).transpose(0, 2, 1).astype(BF16)
    a = jnp.arange(C, dtype=jnp.int32)
    angc = (2.0 * math.pi / C) * ((a[:, None] * a[None, :]) % C).astype(F32)
    fr, fi = jnp.cos(angc), -jnp.sin(angc)
    wf = jnp.block([[fr, -fi], [fi, fr]]).astype(BF16)
    wi = jnp.block([[fr, fi], [-fi, fr]]).astype(BF16)
    return gfwd, ginv, wf, wi, N


def _pos_features(L, P):
    bands = (POS_EMB_DIM - 1) // 2
    i = jnp.arange(P, dtype=F32)
    t = jnp.linspace(0.0, 1.0, L, dtype=F32)
    t = jnp.concatenate([t, jnp.ones((P - L,), F32)])[:, None]
    w = (2.0 * math.pi / L) * i[:, None]
    f = jnp.linspace(1e-4, bands - 1, bands, dtype=F32)[None, :]
    z = jnp.concatenate([t, jnp.cos(f * w), -jnp.sin(f * w)], axis=-1)
    return jnp.pad(z, ((0, 0), (0, LANE - POS_EMB_DIM))), t


def hyena_group(uc, L, D, filt, skip):
    P = uc.shape[0]
    R = P // FFT_C
    w1p, b1, w2, b2, w3f, freq, dec = filt
    gfwd, ginv, wf, wi, N = _dft_tables(R)
    zfeat, tcol = _pos_features(L, P)
    hw, ssq = hyena_filter(zfeat, tcol, w1p, b1, w2, b2, w3f, freq, dec, L)
    n_orders = skip.shape[0]
    ssq = jnp.sum(ssq[::SUBLANE], axis=0).reshape(n_orders, 2, D)
    h0 = hw[0].reshape(n_orders, 2, D)
    total = ssq[:, 0] + ssq[:, 1] + 2.0 * h0[:, 0] * h0[:, 1]
    scale = lax.rsqrt(total + EPS) / N
    yh = fft_fwd(hw.reshape(R, FFT_C, hw.shape[1]), 0, hw.shape[1], gfwd)
    uc3 = uc.reshape(R, FFT_C, 3 * D)
    z3 = uc3
    for n in range(n_orders):
        y = fft_fwd(z3, 0, D, gfwd)
        y = spectral_mul(y, yh, n, scale[n:n + 1], wf, wi)
        z3 = fft_inv_gate(y, ginv, z3, 0, uc3, (n + 1) * D, skip[n:n + 1], L)
    return z3.reshape(P, D)


def _rope(x, rot_ref, cos_ref, sin_ref):
    xr = jnp.dot(x.astype(BF16), rot_ref[...], preferred_element_type=F32)
    return x * cos_ref[...] + xr * sin_ref[...]


def _q_head_body(c_ref, gq_ref, w_ref, gh_ref, rot_ref, cos_ref, sin_ref, q_ref, cn_ref, *, q_lora, scale):
    @pl.when(pl.program_id(1) == 0)
    def _():
        cn_ref[...] = _rms(c_ref[:, :q_lora], gq_ref[...]).astype(BF16)

    q = jnp.dot(cn_ref[...], w_ref[0], preferred_element_type=F32)
    q = _rope(_rms(q, gh_ref[...]), rot_ref, cos_ref, sin_ref)
    q_ref[0] = (q * scale).astype(q_ref.dtype)


def mla_q_heads(cqkv, g_q, w_uq_h, g_qhead, rot, cosf, sinf, q_lora, tm=TOK_TILE):
    T, CW = cqkv.shape
    H = w_uq_h.shape[0]
    return pl.pallas_call(
        functools.partial(_q_head_body, q_lora=q_lora, scale=QK_HEAD ** -0.5 * math.log2(math.e)),
        grid=(T // tm, H),
        in_specs=[pl.BlockSpec((tm, CW), lambda i, h: (i, 0)),
                  pl.BlockSpec((1, q_lora), lambda i, h: (0, 0)),
                  pl.BlockSpec((1, q_lora, QK_HEAD), lambda i, h: (h, 0, 0)),
                  pl.BlockSpec((1, QK_HEAD), lambda i, h: (0, 0)),
                  pl.BlockSpec((QK_HEAD, QK_HEAD), lambda i, h: (0, 0)),
                  pl.BlockSpec((tm, QK_HEAD), lambda i, h: (i, 0)),
                  pl.BlockSpec((tm, QK_HEAD), lambda i, h: (i, 0))],
        out_specs=pl.BlockSpec((1, tm, QK_HEAD), lambda i, h: (h, i, 0)),
        out_shape=jax.ShapeDtypeStruct((H, T, QK_HEAD), BF16),
        scratch_shapes=[pltpu.VMEM((tm, q_lora), BF16)],
        compiler_params=_cparams("parallel", "arbitrary"),
        name="mla_q_heads",
    )(cqkv, g_q.reshape(1, -1), w_uq_h, g_qhead.reshape(1, -1), rot, cosf, sinf)


def _kv_head_body(c_ref, gkv_ref, wk_ref, wvt_ref, gh_ref, rot_ref, cos_ref, sin_ref, k_ref, vt_ref, cn_ref,
                  *, q_lora, kv_lora):
    @pl.when(pl.program_id(1) == 0)
    def _():
        cn_ref[...] = _rms(c_ref[:, q_lora:q_lora + kv_lora], gkv_ref[...]).astype(BF16)

    cn = cn_ref[...]
    k_nope = jnp.dot(cn, wk_ref[0], preferred_element_type=F32)
    k_pe = c_ref[:, q_lora + kv_lora:q_lora + kv_lora + QK_ROPE]
    k = jnp.concatenate([k_nope, k_pe], axis=-1)
    k = _rope(_rms(k, gh_ref[...]), rot_ref, cos_ref, sin_ref)
    k_ref[0] = k.astype(k_ref.dtype)
    vt = lax.dot_general(wvt_ref[0], cn, (((1,), (1,)), ((), ())), preferred_element_type=F32)
    vt_ref[0, :V_HEAD, :] = vt.astype(vt_ref.dtype)
    extra = vt_ref.shape[1] - V_HEAD
    row = lax.broadcasted_iota(jnp.int32, (extra, vt.shape[1]), 0)
    vt_ref[0, V_HEAD:, :] = jnp.where(row == 0, 1.0, 0.0).astype(vt_ref.dtype)


def mla_kv_heads(cqkv, g_kv, w_uk_h, w_uvt_h, g_khead, rot, cosf, sinf, q_lora, kv_lora, tm=TOK_TILE):
    T, CW = cqkv.shape
    H = w_uk_h.shape[0]
    return pl.pallas_call(
        functools.partial(_kv_head_body, q_lora=q_lora, kv_lora=kv_lora),
        grid=(T // tm, H),
        in_specs=[pl.BlockSpec((tm, CW), lambda i, h: (i, 0)),
                  pl.BlockSpec((1, kv_lora), lambda i, h: (0, 0)),
                  pl.BlockSpec((1, kv_lora, QK_NOPE), lambda i, h: (h, 0, 0)),
                  pl.BlockSpec((1, V_HEAD, kv_lora), lambda i, h: (h, 0, 0)),
                  pl.BlockSpec((1, QK_HEAD), lambda i, h: (0, 0)),
                  pl.BlockSpec((QK_HEAD, QK_HEAD), lambda i, h: (0, 0)),
                  pl.BlockSpec((tm, QK_HEAD), lambda i, h: (i, 0)),
                  pl.BlockSpec((tm, QK_HEAD), lambda i, h: (i, 0))],
        out_specs=[pl.BlockSpec((1, tm, QK_HEAD), lambda i, h: (h, i, 0)),
                   pl.BlockSpec((1, VT_ROWS, tm), lambda i, h: (h, 0, i))],
        out_shape=[jax.ShapeDtypeStruct((H, T, QK_HEAD), BF16),
                   jax.ShapeDtypeStruct((H, VT_ROWS, T), BF16)],
        scratch_shapes=[pltpu.VMEM((tm, kv_lora), BF16)],
        compiler_params=_cparams("parallel", "arbitrary"),
        name="mla_kv_heads",
    )(cqkv, g_kv.reshape(1, -1), w_uk_h, w_uvt_h, g_khead.reshape(1, -1), rot, cosf, sinf)


def _flash_body(q_ref, k_hbm, vt_hbm, o_prev_ref, o_ref, k_buf, vt_buf, acc_ref, s_buf, p_buf, sem,
                *, row0, n_chunks, last_valid, tk):
    del o_prev_ref
    h = pl.program_id(0)
    n_keys = k_buf.shape[0]

    @pl.when(pl.program_id(1) == 0)
    def _():
        ck = pltpu.make_async_copy(k_hbm.at[h, pl.ds(row0, n_keys)], k_buf, sem.at[0])
        cv = pltpu.make_async_copy(vt_hbm.at[h, :, pl.ds(row0, n_keys)], vt_buf, sem.at[1])
        ck.start()
        cv.start()
        ck.wait()
        cv.wait()

    q = q_ref[0]
    tq = q.shape[0]
    acc_ref[...] = jnp.zeros_like(acc_ref)

    def key_start(c):
        return c * tk if isinstance(c, int) else pl.multiple_of(c * tk, tk)

    def qk(c, slot):
        s_buf[slot] = lax.dot_general(k_buf[pl.ds(key_start(c), tk), :], q, (((1,), (1,)), ((), ())),
                                      preferred_element_type=F32)

    def softmax(slot, m, n_valid=None):
        s = s_buf[slot]
        if n_valid is not None:
            s = jnp.where(lax.broadcasted_iota(jnp.int32, s.shape, 0) < n_valid, s, NEG_BIG)
        m_new = jnp.maximum(m, jnp.max(s, axis=0, keepdims=True))
        p_buf[slot] = jnp.exp2(s - m_new).astype(BF16)
        return m_new, jnp.exp2(m - m_new)

    def pv(c, slot, alpha):
        acc_ref[...] = alpha * acc_ref[...] + jnp.dot(vt_buf[:, pl.ds(key_start(c), tk)], p_buf[slot],
                                                      preferred_element_type=F32)

    def step(c, par, m, alpha):
        pv(c, par, alpha)
        m, alpha = softmax(1 - par, m)
        qk(c + 2, par)
        return m, alpha

    def pair(i, carry):
        m, alpha = step(2 * i, 0, *carry)
        return step(2 * i + 1, 1, m, alpha)

    n = n_chunks
    qk(0, 0)
    qk(1, 1)
    m, alpha = softmax(0, jnp.full((1, tq), NEG_BIG, F32))
    m, alpha = lax.fori_loop(0, (n - 2) // 2, pair, (m, alpha))
    if (n - 2) % 2:
        m, alpha = step(n - 3, 0, m, alpha)
    par = (n - 2) % 2
    pv(n - 2, par, alpha)
    m, alpha = softmax(1 - par, m, last_valid)
    pv(n - 1, 1 - par, alpha)
    acc = acc_ref[...]
    o = acc[:V_HEAD] / acc[V_HEAD:V_HEAD + 1]
    o_ref[...] = o.T.astype(o_ref.dtype)


def flash_attention(q, k, vt, o_prev, row0, n_rows, n_valid, tq=512, tk=512):
    H, T, _ = q.shape
    q0 = row0 // tq
    n_chunks = -(-n_valid // tk)
    last_valid = n_valid - (n_chunks - 1) * tk
    assert n_chunks >= 3 and n_chunks * tk <= n_rows
    return pl.pallas_call(
        functools.partial(_flash_body, row0=row0, n_chunks=n_chunks, last_valid=last_valid, tk=tk),
        grid=(H, n_rows // tq),
        in_specs=[pl.BlockSpec((1, tq, QK_HEAD), lambda h, i: (h, q0 + i, 0)),
                  pl.BlockSpec(memory_space=pl.ANY),
                  pl.BlockSpec(memory_space=pl.ANY),
                  pl.BlockSpec(memory_space=pl.ANY)],
        out_specs=pl.BlockSpec((tq, V_HEAD), lambda h, i: (q0 + i, h)),
        out_shape=jax.ShapeDtypeStruct((T, H * V_HEAD), BF16),
        scratch_shapes=[pltpu.VMEM((n_rows, QK_HEAD), BF16), pltpu.VMEM((VT_ROWS, n_rows), BF16),
                        pltpu.VMEM((VT_ROWS, tq), F32), pltpu.VMEM((2, tk, tq), F32), pltpu.VMEM((2, tk, tq), BF16),
                        pltpu.SemaphoreType.DMA((2,))],
        input_output_aliases={3: 0},
        compiler_params=_cparams("arbitrary", "arbitrary"),
        name="flash_attention",
    )(q, k, vt, o_prev)


META_E1, META_E2, META_G1, META_G2, META_R1, META_R2 = range(6)


def _router_body(h_ref, g_ref, wr_ref, xn_ref, meta_ref, cnt_ref, carry_ref, *, token_ranges):
    i = pl.program_id(0)

    @pl.when(i == 0)
    def _():
        carry_ref[...] = jnp.zeros_like(carry_ref)

    xn = _rms(h_ref[...], g_ref[...])
    xn_ref[...] = xn
    logits = jnp.dot(xn, wr_ref[...], precision=lax.Precision.HIGHEST, preferred_element_type=F32)
    tb = logits.shape[0]
    lane = lax.broadcasted_iota(jnp.int32, logits.shape, 1)
    lg = jnp.where(lane < N_EXPERTS, logits, NEG_BIG)
    m1 = jnp.max(lg, axis=-1, keepdims=True)
    i1 = jnp.min(jnp.where(lg == m1, lane, LANE), axis=-1, keepdims=True)
    lg2 = jnp.where(lane == i1, NEG_BIG, lg)
    m2 = jnp.max(lg2, axis=-1, keepdims=True)
    i2 = jnp.min(jnp.where(lg2 == m2, lane, LANE), axis=-1, keepdims=True)
    e = jnp.exp(m2 - m1)
    row = lax.broadcasted_iota(jnp.int32, (tb, 1), 0) + i * tb
    valid = functools.reduce(jnp.logical_or, [(row >= a) & (row < b) for a, b in token_ranges])
    g1 = jnp.where(valid, 1.0 / (1.0 + e), 0.0)
    g2 = jnp.where(valid, e / (1.0 + e), 0.0)
    oh = jnp.where(valid & ((lane == i1) | (lane == i2)), 1.0, 0.0)
    r = lax.broadcasted_iota(jnp.int32, (tb, tb), 0)
    c = lax.broadcasted_iota(jnp.int32, (tb, tb), 1)
    tri = jnp.where(c < r, 1.0, 0.0).astype(BF16)
    pre = jnp.dot(tri, oh.astype(BF16), preferred_element_type=F32) + carry_ref[0:1, :]
    r1 = jnp.sum(jnp.where(lane == i1, pre, 0.0), axis=-1, keepdims=True)
    r2 = jnp.sum(jnp.where(lane == i2, pre, 0.0), axis=-1, keepdims=True)
    carry_ref[...] += jnp.broadcast_to(jnp.sum(oh, axis=0, keepdims=True), carry_ref.shape)
    meta = jnp.zeros(logits.shape, F32)
    for idx, val in ((META_E1, i1.astype(F32)), (META_E2, i2.astype(F32)), (META_G1, g1), (META_G2, g2),
                     (META_R1, r1), (META_R2, r2)):
        meta = jnp.where(lane == idx, val, meta)
    meta_ref[...] = meta
    cnt_ref[...] = carry_ref[...]


def moe_router(h, g, w_router_pad, token_ranges, tb=256):
    T, D = h.shape
    return pl.pallas_call(
        functools.partial(_router_body, token_ranges=token_ranges),
        grid=(T // tb,),
        in_specs=[pl.BlockSpec((tb, D), lambda i: (i, 0)),
                  pl.BlockSpec((1, D), lambda i: (0, 0)),
                  pl.BlockSpec((D, LANE), lambda i: (0, 0))],
        out_specs=[pl.BlockSpec((tb, D), lambda i: (i, 0)),
                   pl.BlockSpec((tb, LANE), lambda i: (i, 0)),
                   pl.BlockSpec((SUBLANE, LANE), lambda i: (0, 0))],
        out_shape=[jax.ShapeDtypeStruct((T, D), F32),
                   jax.ShapeDtypeStruct((T, LANE), F32),
                   jax.ShapeDtypeStruct((SUBLANE, LANE), F32)],
        scratch_shapes=[pltpu.VMEM((SUBLANE, LANE), F32)],
        compiler_params=_cparams("arbitrary"),
        name="moe_router",
    )(h, g.reshape(1, D), w_router_pad)


def _expert_up_body(te_ref, nu_ref, src_ref, xn_hbm, wg_ref, wu_ref, o_ref, xf_ref, xb_ref, sem):
    active = pl.program_id(0) < nu_ref[0]
    tm = xf_ref.shape[0]

    @pl.when(active & (pl.program_id(1) == 0))
    def _():
        def row_copy(r):
            return pltpu.make_async_copy(xn_hbm.at[pl.ds(src_ref[0, 0, r], 1)], xf_ref.at[pl.ds(r, 1)], sem)

        def start(r, c):
            row_copy(r).start()
            return c

        def wait(r, c):
            row_copy(r).wait()
            return c

        lax.fori_loop(0, tm, start, 0, unroll=DMA_UNROLL)
        lax.fori_loop(0, tm, wait, 0, unroll=DMA_UNROLL)
        xb_ref[...] = xf_ref[...].astype(BF16)

    @pl.when(active)
    def _():
        x = xb_ref[...]
        a = jnp.dot(x, wg_ref[0], preferred_element_type=F32)
        b = jnp.dot(x, wu_ref[0], preferred_element_type=F32)
        o_ref[...] = _silu_mul(a, b).astype(o_ref.dtype)

    @pl.when(jnp.logical_not(active))
    def _():
        o_ref[...] = jnp.zeros_like(o_ref)


def expert_up(xn, src, wg, wu, tile_expert, n_used, tm, tf=1024):
    D = xn.shape[1]
    n_tiles = src.shape[0] // tm
    F = wg.shape[2]
    tf = _tile(F, tf)
    grid_spec = pltpu.PrefetchScalarGridSpec(
        num_scalar_prefetch=2,
        grid=(n_tiles, F // tf),
        in_specs=[pl.BlockSpec((1, 1, tm), lambda i, j, te, nu: (i, 0, 0), memory_space=pltpu.SMEM),
                  pl.BlockSpec(memory_space=pl.ANY),
                  pl.BlockSpec((1, D, tf), lambda i, j, te, nu: (te[i], 0, j)),
                  pl.BlockSpec((1, D, tf), lambda i, j, te, nu: (te[i], 0, j))],
        out_specs=pl.BlockSpec((tm, tf), lambda i, j, te, nu: (i, j)),
        scratch_shapes=[pltpu.VMEM((tm, D), F32), pltpu.VMEM((tm, D), BF16), pltpu.SemaphoreType.DMA(())],
    )
    return pl.pallas_call(
        _expert_up_body,
        grid_spec=grid_spec,
        out_shape=jax.ShapeDtypeStruct((n_tiles * tm, F), BF16),
        compiler_params=_cparams("arbitrary", "arbitrary"),
        name="moe_expert_up",
    )(tile_expert, n_used, src.reshape(n_tiles, 1, tm), xn, wg, wu)


def _expert_down_body(te_ref, nu_ref, a_ref, w_ref, o_ref):
    @pl.when(pl.program_id(0) < nu_ref[0])
    def _():
        o_ref[...] = jnp.dot(a_ref[...], w_ref[0], preferred_element_type=F32)

    @pl.when(pl.program_id(0) >= nu_ref[0])
    def _():
        o_ref[...] = jnp.zeros_like(o_ref)


def expert_down(act, wd, tile_expert, n_used, tm, tn=512):
    S, F = act.shape
    D = wd.shape[2]
    tn = _tile(D, tn)
    grid_spec = pltpu.PrefetchScalarGridSpec(
        num_scalar_prefetch=2,
        grid=(S // tm, D // tn),
        in_specs=[pl.BlockSpec((tm, F), lambda i, j, te, nu: (i, 0)),
                  pl.BlockSpec((1, F, tn), lambda i, j, te, nu: (te[i], 0, j))],
        out_specs=pl.BlockSpec((tm, tn), lambda i, j, te, nu: (i, j)),
    )
    return pl.pallas_call(
        _expert_down_body,
        grid_spec=grid_spec,
        out_shape=jax.ShapeDtypeStruct((S, D), F32),
        compiler_params=_cparams("parallel", "parallel"),
        name="moe_expert_down",
    )(tile_expert, n_used, act, wd)


def _combine_body(pos_ref, h_ref, meta_ref, ys_hbm, o_ref, buf, sem, *, tb):
    def row_copy(r, k):
        return pltpu.make_async_copy(ys_hbm.at[pl.ds(pos_ref[0, 0, 2 * r + k], 1)],
                                     buf.at[k, pl.ds(r, 1)], sem)

    def start(r, c):
        row_copy(r, 0).start()
        row_copy(r, 1).start()
        return c

    def wait(r, c):
        row_copy(r, 0).wait()
        row_copy(r, 1).wait()
        return c

    lax.fori_loop(0, tb, start, 0, unroll=DMA_UNROLL)
    lax.fori_loop(0, tb, wait, 0, unroll=DMA_UNROLL)
    g1 = meta_ref[:, META_G1:META_G1 + 1]
    g2 = meta_ref[:, META_G2:META_G2 + 1]
    o_ref[...] = h_ref[...] + g1 * buf[0] + g2 * buf[1]


def moe_combine(h, meta, pos, ys, tb=256):
    T, D = h.shape
    pos3 = pos.reshape(T // tb, 1, 2 * tb)
    return pl.pallas_call(
        functools.partial(_combine_body, tb=tb),
        grid=(T // tb,),
        in_specs=[pl.BlockSpec((1, 1, 2 * tb), lambda i: (i, 0, 0), memory_space=pltpu.SMEM),
                  pl.BlockSpec((tb, D), lambda i: (i, 0)),
                  pl.BlockSpec((tb, LANE), lambda i: (i, 0)),
                  pl.BlockSpec(memory_space=pl.ANY)],
        out_specs=pl.BlockSpec((tb, D), lambda i: (i, 0)),
        out_shape=jax.ShapeDtypeStruct((T, D), F32),
        scratch_shapes=[pltpu.VMEM((2, tb, D), F32), pltpu.SemaphoreType.DMA(())],
        compiler_params=_cparams("arbitrary"),
        name="moe_combine",
    )(pos3, h, meta, ys)


def moe_layer(h, g, w_router, wg, wu, wd, token_ranges, tm=512):
    T, D = h.shape
    wr = jnp.pad(w_router, ((0, 0), (0, LANE - N_EXPERTS)))
    xn, meta, cnt = moe_router(h, g, wr, token_ranges)
    counts = cnt[0, :N_EXPERTS].astype(jnp.int32)
    tiles = (counts + tm - 1) // tm
    tile_end = jnp.cumsum(tiles)
    starts = (tile_end - tiles) * tm
    n_tok = sum(b - a for a, b in token_ranges)
    n_tiles = (2 * n_tok + N_EXPERTS * (tm - 1)) // tm
    e_idx = meta[:, META_E1:META_E2 + 1].astype(jnp.int32)
    rank = meta[:, META_R1:META_R2 + 1].astype(jnp.int32)
    row = jnp.arange(T, dtype=jnp.int32)[:, None]
    valid = functools.reduce(jnp.logical_or, [(row >= a) & (row < b) for a, b in token_ranges])
    pos = starts[e_idx] + rank
    tile_expert = jnp.minimum(jnp.sum(jnp.arange(n_tiles)[:, None] >= tile_end[None, :], axis=1),
                              N_EXPERTS - 1).astype(jnp.int32)
    n_used = tile_end[-1:].astype(jnp.int32)
    tok = jnp.repeat(row[:, 0], 2)
    src = jnp.zeros((n_tiles * tm,), jnp.int32).at[jnp.where(valid, pos, n_tiles * tm).reshape(-1)].set(
        tok, mode="drop")
    act = expert_up(xn, src, wg, wu, tile_expert, n_used, tm)
    ys = expert_down(act, wd, tile_expert, n_used, tm)
    return moe_combine(h, meta, jnp.where(valid, pos, 0), ys)


def _padded_len(L):
    r = -(-L // FFT_C)
    r_align = TOK_TILE // FFT_C
    r = -(-r // r_align) * r_align
    return r * FFT_C


def _rope_tables(lens, pads):
    half = QK_ROPE // 2
    inv = 1.0 / (ROPE_BASE ** (jnp.arange(0, QK_ROPE, 2, dtype=F32) / QK_ROPE))
    cs, ss = [], []
    for L, P in zip(lens, pads):
        ang = jnp.arange(P, dtype=F32)[:, None] * inv[None, :]
        c, s = jnp.cos(ang), jnp.sin(ang)
        cs.append(jnp.concatenate([jnp.ones((P, QK_NOPE), F32), c, c], axis=-1))
        ss.append(jnp.concatenate([jnp.zeros((P, QK_NOPE), F32), s, s], axis=-1))
    rot = np.zeros((QK_HEAD, QK_HEAD), np.float32)
    for c in range(QK_NOPE, QK_NOPE + half):
        rot[c + half, c] = -1.0
        rot[c, c + half] = 1.0
    return jnp.concatenate(cs), jnp.concatenate(ss), jnp.asarray(rot, BF16)


def kernel(x_prompt, x_sample, meta_tokens, norm_mix, norm_ffn, hy_w_in, hy_conv_w, hy_conv_b, hy_filt_w1, hy_filt_b1, hy_filt_w2, hy_filt_b2, hy_filt_w3, hy_filt_freq, hy_filt_log_decay, hy_skip, hy_w_out, ffn_w_gate, ffn_w_up, ffn_w_down, mla_w_dq, mla_g_q, mla_w_uq, mla_w_dkv, mla_g_kv, mla_w_ukv, mla_g_qhead, mla_g_khead, mla_w_o, moe_w_router, moe_w_gate, moe_w_up, moe_w_down):
    D = x_prompt.shape[-1]
    n_meta = meta_tokens.shape[0]
    assert x_prompt.shape[0] == 1 and x_sample.shape[0] == 1
    groups = (x_prompt[0], x_sample[0])
    lens = [n_meta + g.shape[0] for g in groups]
    pads = [_padded_len(L) for L in lens]
    row0s = [0, pads[0]]
    T = sum(pads)
    assert T % TOK_TILE == 0 and all(p % TOK_TILE == 0 for p in pads)

    parts = []
    for g, L, P in zip(groups, lens, pads):
        parts += [meta_tokens.astype(F32), g, jnp.zeros((P - L, D), F32)]
    h = jnp.concatenate(parts, axis=0)

    bf = lambda w: w.astype(BF16)

    u = norm_mm(h, norm_mix[0], [bf(hy_w_in)], lambda y: y, BF16, 1024, "hyena_in_proj")
    n_orders = hy_skip.shape[0]
    fh = hy_filt_w2.shape[0]
    filt = (jnp.pad(hy_filt_w1, ((0, LANE - POS_EMB_DIM), (0, 0))), hy_filt_b1, hy_filt_w2, hy_filt_b2,
            bf(hy_filt_w3.reshape(fh, n_orders * 2 * D)), hy_filt_freq, hy_filt_log_decay.reshape(-1))
    zs = []
    for L, P, r0 in zip(lens, pads, row0s):
        uc = short_conv(u, hy_conv_w, hy_conv_b, r0, P, L)
        zs.append(hyena_group(uc, L, D, filt, hy_skip))
    z = jnp.concatenate(zs, axis=0)
    h = mm_res(z, bf(hy_w_out), h, 1024, "hyena_out_proj")

    a = norm_mm(h, norm_ffn[0], [bf(ffn_w_gate), bf(ffn_w_up)], _silu_mul, BF16, 1024, "ffn_swiglu")
    h = mm_res(a, bf(ffn_w_down), h, 512, "ffn_down")

    q_lora = mla_w_dq.shape[1]
    kv_lora = mla_w_ukv.shape[0]
    cw = q_lora + mla_w_dkv.shape[1]
    cw_pad = -(-cw // LANE) * LANE
    w_down = jnp.pad(jnp.concatenate([mla_w_dq, mla_w_dkv], axis=1), ((0, 0), (0, cw_pad - cw)))
    cqkv = norm_mm(h, norm_mix[1], [bf(w_down)], lambda y: y, F32, cw_pad, "mla_down_proj")
    cosf, sinf, rot = _rope_tables(lens, pads)
    w_uq_h = bf(mla_w_uq.reshape(q_lora, MLA_HEADS, QK_HEAD).transpose(1, 0, 2))
    w_ukv_h = mla_w_ukv.reshape(kv_lora, MLA_HEADS, QK_NOPE + V_HEAD).transpose(1, 0, 2)
    w_uk_h = bf(w_ukv_h[:, :, :QK_NOPE])
    w_uvt_h = bf(w_ukv_h[:, :, QK_NOPE:].transpose(0, 2, 1))
    q = mla_q_heads(cqkv, mla_g_q, w_uq_h, mla_g_qhead, rot, cosf, sinf, q_lora)
    k, vt = mla_kv_heads(cqkv, mla_g_kv, w_uk_h, w_uvt_h, mla_g_khead, rot, cosf, sinf, q_lora, kv_lora)
    o = jnp.zeros((T, MLA_HEADS * V_HEAD), BF16)
    for L, P, r0 in zip(lens, pads, row0s):
        o = flash_attention(q, k, vt, o, r0, P, L)
    h = mm_res(o, bf(mla_w_o), h, 1024, "mla_out_proj")

    token_ranges = tuple((r0, r0 + L) for r0, L in zip(row0s, lens))
    h = moe_layer(h, norm_ffn[1], moe_w_router, bf(moe_w_gate), bf(moe_w_up), bf(moe_w_down), token_ranges)

    outs = []
    for g, r0 in zip(groups, row0s):
        outs.append(h[r0 + n_meta:r0 + n_meta + g.shape[0]][None])
    return tuple(outs)
```

```python
import functools
import math

import numpy as np
import jax
import jax.numpy as jnp
from jax import lax
from jax.experimental import pallas as pl
from jax.experimental.pallas import tpu as pltpu

F32 = jnp.float32
BF16 = jnp.bfloat16

EPS = 1e-6
MLA_HEADS = 16
QK_NOPE = 128
QK_ROPE = 64
QK_HEAD = QK_NOPE + QK_ROPE
V_HEAD = 128
VT_ROWS = V_HEAD + 16
ROPE_BASE = 10000.0
N_EXPERTS = 8
POS_EMB_DIM = 33

LANE = 128
SUBLANE = 8
VMEM_LIMIT_BYTES = 56 * 1024 * 1024

FFT_C = 128
FFT_JB = 2 * SUBLANE
FFT_KB = SUBLANE

TOK_TILE = 512
DMA_UNROLL = 8
NEG_BIG = -1e30


def _cparams(*sem):
    return pltpu.CompilerParams(dimension_semantics=sem, vmem_limit_bytes=VMEM_LIMIT_BYTES)


def _tile(n, pref):
    if n <= pref:
        return n
    t = pref - pref % LANE
    while n % t:
        t -= LANE
    return t


def _rms(x, g):
    ms = jnp.mean(x * x, axis=-1, keepdims=True)
    return x * lax.rsqrt(ms + EPS) * g


def _norm_mm_body(x_ref, g_ref, *rest, n_w, epilogue):
    w_refs, o_ref, xn_ref = rest[:n_w], rest[n_w], rest[n_w + 1]

    @pl.when(pl.program_id(1) == 0)
    def _():
        xn_ref[...] = _rms(x_ref[...], g_ref[...]).astype(BF16)

    xn = xn_ref[...]
    ys = [jnp.dot(xn, w[...], preferred_element_type=F32) for w in w_refs]
    o_ref[...] = epilogue(*ys).astype(o_ref.dtype)


def norm_mm(x, g, ws, epilogue, out_dtype, tn, name, tm=TOK_TILE):
    T, K = x.shape
    N = ws[0].shape[1]
    tn = _tile(N, tn)
    n_w = len(ws)
    return pl.pallas_call(
        functools.partial(_norm_mm_body, n_w=n_w, epilogue=epilogue),
        grid=(T // tm, N // tn),
        in_specs=[pl.BlockSpec((tm, K), lambda i, j: (i, 0)),
                  pl.BlockSpec((1, K), lambda i, j: (0, 0))]
        + [pl.BlockSpec((K, tn), lambda i, j: (0, j))] * n_w,
        out_specs=pl.BlockSpec((tm, tn), lambda i, j: (i, j)),
        out_shape=jax.ShapeDtypeStruct((T, N), out_dtype),
        scratch_shapes=[pltpu.VMEM((tm, K), BF16)],
        compiler_params=_cparams("parallel", "arbitrary"),
        name=name,
    )(x, g.reshape(1, K), *ws)


def _mm_res_body(x_ref, w_ref, r_ref, o_ref):
    o_ref[...] = r_ref[...] + jnp.dot(x_ref[...].astype(BF16), w_ref[...], preferred_element_type=F32)


def mm_res(x, w, res, tn, name, tm=TOK_TILE):
    T, K = x.shape
    N = w.shape[1]
    tn = _tile(N, tn)
    return pl.pallas_call(
        _mm_res_body,
        grid=(T // tm, N // tn),
        in_specs=[pl.BlockSpec((tm, K), lambda i, j: (i, 0)),
                  pl.BlockSpec((K, tn), lambda i, j: (0, j)),
                  pl.BlockSpec((tm, tn), lambda i, j: (i, j))],
        out_specs=pl.BlockSpec((tm, tn), lambda i, j: (i, j)),
        out_shape=jax.ShapeDtypeStruct((T, N), F32),
        compiler_params=_cparams("parallel", "parallel"),
        name=name,
    )(x, w, res)


def _silu_mul(a, b):
    return a * (1.0 / (1.0 + jnp.exp(-a))) * b


def _short_conv_body(u_ref, prev_ref, next_ref, w_ref, b_ref, o_ref, *, row0, n_valid, n_rows_total, halo):
    i = pl.program_id(0)
    tm = u_ref.shape[0]
    u = u_ref[...].astype(F32)
    row = lax.broadcasted_iota(jnp.int32, u.shape, 0)
    g_row = row0 + i * tm
    prev_row = jnp.where(g_row == 0, 0.0, prev_ref[...].astype(F32)[halo - 1:halo, :])
    next_row = jnp.where(g_row + tm == n_rows_total, 0.0, next_ref[...].astype(F32)[0:1, :])
    up = jnp.where(row == 0, prev_row, pltpu.roll(u, 1, 0))
    dn = jnp.where(row == tm - 1, next_row, pltpu.roll(u, tm - 1, 0))
    y = b_ref[...] + up * w_ref[0:1, :] + u * w_ref[1:2, :] + dn * w_ref[2:3, :]
    o_ref[...] = jnp.where(i * tm + row < n_valid, y, 0.0)


def short_conv(u, w, b, row0, n_rows, n_valid, tm=TOK_TILE, tn=1024):
    T, C3 = u.shape
    tn = _tile(C3, tn)
    halo = 16
    r0 = row0 // tm
    hb = tm // halo
    nhalo = T // halo
    return pl.pallas_call(
        functools.partial(_short_conv_body, row0=row0, n_valid=n_valid, n_rows_total=T, halo=halo),
        grid=(n_rows // tm, C3 // tn),
        in_specs=[pl.BlockSpec((tm, tn), lambda i, j: (r0 + i, j)),
                  pl.BlockSpec((halo, tn), lambda i, j: (jnp.maximum((r0 + i) * hb - 1, 0), j)),
                  pl.BlockSpec((halo, tn), lambda i, j: (jnp.minimum((r0 + i + 1) * hb, nhalo - 1), j)),
                  pl.BlockSpec((3, tn), lambda i, j: (0, j)),
                  pl.BlockSpec((1, tn), lambda i, j: (0, j))],
        out_specs=pl.BlockSpec((tm, tn), lambda i, j: (i, j)),
        out_shape=jax.ShapeDtypeStruct((n_rows, C3), F32),
        compiler_params=_cparams("parallel", "parallel"),
        name="hyena_short_conv",
    )(u, u, u, w, b.reshape(1, C3))


def _filter_body(z_ref, w1_ref, b1_ref, w2_ref, b2_ref, w3_ref, fr_ref, dec_ref, t_ref,
                 h_ref, ssq_ref, hdn_ref, *, n_valid):
    i = pl.program_id(0)
    tm = z_ref.shape[0]

    @pl.when(pl.program_id(1) == 0)
    def _():
        hi = lax.Precision.HIGHEST
        a = jnp.dot(z_ref[...], w1_ref[...], precision=hi, preferred_element_type=F32) + b1_ref[...]
        a = jnp.sin(fr_ref[0:1, :] * a)
        a = jnp.dot(a, w2_ref[...], precision=hi, preferred_element_type=F32) + b2_ref[...]
        hdn_ref[...] = jnp.sin(fr_ref[1:2, :] * a).astype(BF16)

    h = jnp.dot(hdn_ref[...], w3_ref[...], preferred_element_type=F32)
    window = jnp.exp(-t_ref[...] * jnp.exp(dec_ref[...]))
    row = lax.broadcasted_iota(jnp.int32, h.shape, 0) + i * tm
    h = jnp.where(row < n_valid, h * window, 0.0)
    h_ref[...] = h
    ssq_ref[...] = jnp.broadcast_to(jnp.sum(h * h, axis=0, keepdims=True), ssq_ref.shape)


def hyena_filter(zfeat, tcol, w1, b1, w2, b2, w3, freq, log_decay, n_valid, tm=512, tc=2048):
    P = zfeat.shape[0]
    FH, NC = w3.shape
    tc = _tile(NC, tc)
    return pl.pallas_call(
        functools.partial(_filter_body, n_valid=n_valid),
        grid=(P // tm, NC // tc),
        in_specs=[pl.BlockSpec((tm, LANE), lambda i, c: (i, 0)),
                  pl.BlockSpec((LANE, FH), lambda i, c: (0, 0)),
                  pl.BlockSpec((1, FH), lambda i, c: (0, 0)),
                  pl.BlockSpec((FH, FH), lambda i, c: (0, 0)),
                  pl.BlockSpec((1, FH), lambda i, c: (0, 0)),
                  pl.BlockSpec((FH, tc), lambda i, c: (0, c)),
                  pl.BlockSpec((2, FH), lambda i, c: (0, 0)),
                  pl.BlockSpec((1, tc), lambda i, c: (0, c)),
                  pl.BlockSpec((tm, 1), lambda i, c: (i, 0))],
        out_specs=[pl.BlockSpec((tm, tc), lambda i, c: (i, c)),
                   pl.BlockSpec((SUBLANE, tc), lambda i, c: (i, c))],
        out_shape=[jax.ShapeDtypeStruct((P, NC), F32),
                   jax.ShapeDtypeStruct((P // tm * SUBLANE, NC), F32)],
        scratch_shapes=[pltpu.VMEM((tm, FH), BF16)],
        compiler_params=_cparams("parallel", "arbitrary"),
        name="hyena_filter",
    )(zfeat, w1, b1.reshape(1, FH), w2, b2.reshape(1, FH), w3, freq, log_decay.reshape(1, NC), tcol)


def _fft_fwd_body(x_ref, g_ref, y_ref):
    xt = jnp.swapaxes(x_ref[...], 0, 1).astype(BF16)
    ys = [jnp.dot(g_ref[j], xt[j], preferred_element_type=F32) for j in range(FFT_JB)]
    y = jnp.swapaxes(jnp.stack(ys), 0, 1)
    y_ref[...] = y.reshape(y_ref.shape).astype(y_ref.dtype)


def fft_fwd(x3, col0, ncols, gfwd, td=256):
    R, C, _ = x3.shape
    n1 = gfwd.shape[1] // 2
    td = _tile(ncols, td)
    c0 = col0 // td
    return pl.pallas_call(
        _fft_fwd_body,
        grid=(C // FFT_JB, ncols // td),
        in_specs=[pl.BlockSpec((R, FFT_JB, td), lambda i, j: (0, i, c0 + j)),
                  pl.BlockSpec((FFT_JB, 2 * n1, R), lambda i, j: (i, 0, 0))],
        out_specs=pl.BlockSpec((2, n1, FFT_JB, td), lambda i, j: (0, 0, i, j)),
        out_shape=jax.ShapeDtypeStruct((2, n1, C, ncols), BF16),
        compiler_params=_cparams("parallel", "parallel"),
        name="fft_fwd",
    )(x3, gfwd)


def _spectral_body(y_ref, ya_ref, yb_ref, s_ref, wf_ref, wi_ref, o_ref):
    c, td = y_ref.shape[2], y_ref.shape[3]
    s = s_ref[...]
    for kk in range(FFT_KB):
        x = jnp.dot(wf_ref[...], y_ref[:, kk].reshape(2 * c, td), preferred_element_type=F32)
        sa = jnp.dot(wf_ref[...], ya_ref[:, kk].reshape(2 * c, td), preferred_element_type=F32)
        sb = jnp.dot(wf_ref[...], yb_ref[:, kk].reshape(2 * c, td), preferred_element_type=F32)
        xr, xi = x[:c], x[c:]
        kr, ki = (sa[:c] + sb[:c]) * s, (sa[c:] - sb[c:]) * s
        z = jnp.concatenate([xr * kr - xi * ki, xr * ki + xi * kr], axis=0).astype(BF16)
        o = jnp.dot(wi_ref[...], z, preferred_element_type=F32)
        o_ref[:, kk] = o.reshape(2, c, td).astype(o_ref.dtype)


def spectral_mul(y, yh, order, scale, wf, wi, td=1024):
    _, n1, C, D = y.shape
    td = _tile(D, td)
    ca = (order * 2 * D) // td
    cb = (order * 2 * D + D) // td
    mat = pl.BlockSpec((2 * C, 2 * C), lambda a, j: (0, 0))
    return pl.pallas_call(
        _spectral_body,
        grid=(n1 // FFT_KB, D // td),
        in_specs=[pl.BlockSpec((2, FFT_KB, C, td), lambda a, j: (0, a, 0, j)),
                  pl.BlockSpec((2, FFT_KB, C, td), lambda a, j: (0, a, 0, ca + j)),
                  pl.BlockSpec((2, FFT_KB, C, td), lambda a, j: (0, a, 0, cb + j)),
                  pl.BlockSpec((1, td), lambda a, j: (0, j)),
                  mat, mat],
        out_specs=pl.BlockSpec((2, FFT_KB, C, td), lambda a, j: (0, a, 0, j)),
        out_shape=jax.ShapeDtypeStruct((2, n1, C, D), BF16),
        compiler_params=_cparams("parallel", "parallel"),
        name="spectral_mul",
    )(y, yh, yh, scale, wf, wi)


def _fft_inv_body(y_ref, gi_ref, z_ref, gate_ref, skip_ref, o_ref, *, n_valid):
    i = pl.program_id(0)
    n1c2, jb, td = 2 * y_ref.shape[1], y_ref.shape[2], y_ref.shape[3]
    yt = jnp.swapaxes(y_ref[...].reshape(n1c2, jb, td).astype(F32), 0, 1).astype(BF16)
    convs = [jnp.dot(gi_ref[j], yt[j], preferred_element_type=F32) for j in range(jb)]
    conv = jnp.swapaxes(jnp.stack(convs), 0, 1)
    z = gate_ref[...] * (conv + z_ref[...] * skip_ref[...])
    seq = (lax.broadcasted_iota(jnp.int32, z.shape, 0) * FFT_C
           + lax.broadcasted_iota(jnp.int32, z.shape, 1) + i * jb)
    o_ref[...] = jnp.where(seq < n_valid, z, 0.0)


def fft_inv_gate(y, ginv, z3, zcol0, gate3, gcol0, skip, n_valid, td=256):
    _, n1, C, D = y.shape
    R = ginv.shape[1]
    td = _tile(D, td)
    zc, gc = zcol0 // td, gcol0 // td
    return pl.pallas_call(
        functools.partial(_fft_inv_body, n_valid=n_valid),
        grid=(C // FFT_JB, D // td),
        in_specs=[pl.BlockSpec((2, n1, FFT_JB, td), lambda i, j: (0, 0, i, j)),
                  pl.BlockSpec((FFT_JB, R, 2 * n1), lambda i, j: (i, 0, 0)),
                  pl.BlockSpec((R, FFT_JB, td), lambda i, j: (0, i, zc + j)),
                  pl.BlockSpec((R, FFT_JB, td), lambda i, j: (0, i, gc + j)),
                  pl.BlockSpec((1, 1, td), lambda i, j: (0, 0, j))],
        out_specs=pl.BlockSpec((R, FFT_JB, td), lambda i, j: (0, i, j)),
        out_shape=jax.ShapeDtypeStruct((R, C, D), F32),
        compiler_params=_cparams("parallel", "parallel"),
        name="fft_inv_gate",
    )(y, ginv, z3, gate3, skip.reshape(1, 1, -1))


def _dft_tables(R):
    C = FFT_C
    N = 2 * R * C
    n1_pad = 2 * FFT_KB
    n1c = -(-(R + 1) // n1_pad) * n1_pad
    k1 = jnp.arange(n1c, dtype=jnp.int32)[None, :, None]
    n1 = jnp.arange(R, dtype=jnp.int32)[None, None, :]
    n2 = jnp.arange(C, dtype=jnp.int32)[:, None, None]
    ang = (2.0 * math.pi / N) * ((k1 * (C * n1 + n2)) % N).astype(F32)
    used = (k1 <= R).astype(F32)
    mirror = jnp.where((k1 == 0) | (k1 == R), 1.0, 2.0) * used
    gr, gim = jnp.cos(ang), -jnp.sin(ang)
    gfwd = jnp.concatenate([gr * used, gim * used], axis=1).astype(BF16)
    ginv = jnp.concatenate([gr * mirror, gim * mirror], axis=1).transpose(0, 2, 1).astype(BF16)
    a = jnp.arange(C, dtype=jnp.int32)
    angc = (2.0 * math.pi / C) * ((a[:, None] * a[None, :]) % C).astype(F32)
    fr, fi = jnp.cos(angc), -jnp.sin(angc)
    wf = jnp.block([[fr, -fi], [fi, fr]]).astype(BF16)
    wi = jnp.block([[fr, fi], [-fi, fr]]).astype(BF16)
    return gfwd, ginv, wf, wi, N


def _pos_features(L, P):
    bands = (POS_EMB_DIM - 1) // 2
    i = jnp.arange(P, dtype=F32)
    t = jnp.linspace(0.0, 1.0, L, dtype=F32)
    t = jnp.concatenate([t, jnp.ones((P - L,), F32)])[:, None]
    w = (2.0 * math.pi / L) * i[:, None]
    f = jnp.linspace(1e-4, bands - 1, bands, dtype=F32)[None, :]
    z = jnp.concatenate([t, jnp.cos(f * w), -jnp.sin(f * w)], axis=-1)
    return jnp.pad(z, ((0, 0), (0, LANE - POS_EMB_DIM))), t


def hyena_group(uc, L, D, filt, skip):
    P = uc.shape[0]
    R = P // FFT_C
    w1p, b1, w2, b2, w3f, freq, dec = filt
    gfwd, ginv, wf, wi, N = _dft_tables(R)
    zfeat, tcol = _pos_features(L, P)
    hw, ssq = hyena_filter(zfeat, tcol, w1p, b1, w2, b2, w3f, freq, dec, L)
    n_orders = skip.shape[0]
    ssq = jnp.sum(ssq[::SUBLANE], axis=0).reshape(n_orders, 2, D)
    h0 = hw[0].reshape(n_orders, 2, D)
    total = ssq[:, 0] + ssq[:, 1] + 2.0 * h0[:, 0] * h0[:, 1]
    scale = lax.rsqrt(total + EPS) / N
    yh = fft_fwd(hw.reshape(R, FFT_C, hw.shape[1]), 0, hw.shape[1], gfwd)
    uc3 = uc.reshape(R, FFT_C, 3 * D)
    z3 = uc3
    for n in range(n_orders):
        y = fft_fwd(z3, 0, D, gfwd)
        y = spectral_mul(y, yh, n, scale[n:n + 1], wf, wi)
        z3 = fft_inv_gate(y, ginv, z3, 0, uc3, (n + 1) * D, skip[n:n + 1], L)
    return z3.reshape(P, D)


def _rope(x, rot_ref, cos_ref, sin_ref):
    xr = jnp.dot(x.astype(BF16), rot_ref[...], preferred_element_type=F32)
    return x * cos_ref[...] + xr * sin_ref[...]


def _q_head_body(c_ref, gq_ref, w_ref, gh_ref, rot_ref, cos_ref, sin_ref, q_ref, cn_ref, *, q_lora, scale):
    @pl.when(pl.program_id(1) == 0)
    def _():
        cn_ref[...] = _rms(c_ref[:, :q_lora], gq_ref[...]).astype(BF16)

    q = jnp.dot(cn_ref[...], w_ref[0], preferred_element_type=F32)
    q = _rope(_rms(q, gh_ref[...]), rot_ref, cos_ref, sin_ref)
    q_ref[0] = (q * scale).astype(q_ref.dtype)


def mla_q_heads(cqkv, g_q, w_uq_h, g_qhead, rot, cosf, sinf, q_lora, tm=TOK_TILE):
    T, CW = cqkv.shape
    H = w_uq_h.shape[0]
    return pl.pallas_call(
        functools.partial(_q_head_body, q_lora=q_lora, scale=QK_HEAD ** -0.5 * math.log2(math.e)),
        grid=(T // tm, H),
        in_specs=[pl.BlockSpec((tm, CW), lambda i, h: (i, 0)),
                  pl.BlockSpec((1, q_lora), lambda i, h: (0, 0)),
                  pl.BlockSpec((1, q_lora, QK_HEAD), lambda i, h: (h, 0, 0)),
                  pl.BlockSpec((1, QK_HEAD), lambda i, h: (0, 0)),
                  pl.BlockSpec((QK_HEAD, QK_HEAD), lambda i, h: (0, 0)),
                  pl.BlockSpec((tm, QK_HEAD), lambda i, h: (i, 0)),
                  pl.BlockSpec((tm, QK_HEAD), lambda i, h: (i, 0))],
        out_specs=pl.BlockSpec((1, tm, QK_HEAD), lambda i, h: (h, i, 0)),
        out_shape=jax.ShapeDtypeStruct((H, T, QK_HEAD), BF16),
        scratch_shapes=[pltpu.VMEM((tm, q_lora), BF16)],
        compiler_params=_cparams("parallel", "arbitrary"),
        name="mla_q_heads",
    )(cqkv, g_q.reshape(1, -1), w_uq_h, g_qhead.reshape(1, -1), rot, cosf, sinf)


def _kv_head_body(c_ref, gkv_ref, wk_ref, wvt_ref, gh_ref, rot_ref, cos_ref, sin_ref, k_ref, vt_ref, cn_ref,
                  *, q_lora, kv_lora):
    @pl.when(pl.program_id(1) == 0)
    def _():
        cn_ref[...] = _rms(c_ref[:, q_lora:q_lora + kv_lora], gkv_ref[...]).astype(BF16)

    cn = cn_ref[...]
    k_nope = jnp.dot(cn, wk_ref[0], preferred_element_type=F32)
    k_pe = c_ref[:, q_lora + kv_lora:q_lora + kv_lora + QK_ROPE]
    k = jnp.concatenate([k_nope, k_pe], axis=-1)
    k = _rope(_rms(k, gh_ref[...]), rot_ref, cos_ref, sin_ref)
    k_ref[0] = k.astype(k_ref.dtype)
    vt = lax.dot_general(wvt_ref[0], cn, (((1,), (1,)), ((), ())), preferred_element_type=F32)
    vt_ref[0, :V_HEAD, :] = vt.astype(vt_ref.dtype)
    extra = vt_ref.shape[1] - V_HEAD
    row = lax.broadcasted_iota(jnp.int32, (extra, vt.shape[1]), 0)
    vt_ref[0, V_HEAD:, :] = jnp.where(row == 0, 1.0, 0.0).astype(vt_ref.dtype)


def mla_kv_heads(cqkv, g_kv, w_uk_h, w_uvt_h, g_khead, rot, cosf, sinf, q_lora, kv_lora, tm=TOK_TILE):
    T, CW = cqkv.shape
    H = w_uk_h.shape[0]
    return pl.pallas_call(
        functools.partial(_kv_head_body, q_lora=q_lora, kv_lora=kv_lora),
        grid=(T // tm, H),
        in_specs=[pl.BlockSpec((tm, CW), lambda i, h: (i, 0)),
                  pl.BlockSpec((1, kv_lora), lambda i, h: (0, 0)),
                  pl.BlockSpec((1, kv_lora, QK_NOPE), lambda i, h: (h, 0, 0)),
                  pl.BlockSpec((1, V_HEAD, kv_lora), lambda i, h: (h, 0, 0)),
                  pl.BlockSpec((1, QK_HEAD), lambda i, h: (0, 0)),
                  pl.BlockSpec((QK_HEAD, QK_HEAD), lambda i, h: (0, 0)),
                  pl.BlockSpec((tm, QK_HEAD), lambda i, h: (i, 0)),
                  pl.BlockSpec((tm, QK_HEAD), lambda i, h: (i, 0))],
        out_specs=[pl.BlockSpec((1, tm, QK_HEAD), lambda i, h: (h, i, 0)),
                   pl.BlockSpec((1, VT_ROWS, tm), lambda i, h: (h, 0, i))],
        out_shape=[jax.ShapeDtypeStruct((H, T, QK_HEAD), BF16),
                   jax.ShapeDtypeStruct((H, VT_ROWS, T), BF16)],
        scratch_shapes=[pltpu.VMEM((tm, kv_lora), BF16)],
        compiler_params=_cparams("parallel", "arbitrary"),
        name="mla_kv_heads",
    )(cqkv, g_kv.reshape(1, -1), w_uk_h, w_uvt_h, g_khead.reshape(1, -1), rot, cosf, sinf)


def _flash_body(q_ref, k_hbm, vt_hbm, o_prev_ref, o_ref, k_buf, vt_buf, acc_ref, s_buf, p_buf, sem,
                *, row0, n_chunks, last_valid, tk):
    del o_prev_ref
    h = pl.program_id(0)
    n_keys = k_buf.shape[0]

    @pl.when(pl.program_id(1) == 0)
    def _():
        ck = pltpu.make_async_copy(k_hbm.at[h, pl.ds(row0, n_keys)], k_buf, sem.at[0])
        cv = pltpu.make_async_copy(vt_hbm.at[h, :, pl.ds(row0, n_keys)], vt_buf, sem.at[1])
        ck.start()
        cv.start()
        ck.wait()
        cv.wait()

    q = q_ref[0]
    tq = q.shape[0]
    acc_ref[...] = jnp.zeros_like(acc_ref)

    def key_start(c):
        return c * tk if isinstance(c, int) else pl.multiple_of(c * tk, tk)

    def qk(c, slot, size=tk):
        s_buf[slot, :size] = lax.dot_general(k_buf[pl.ds(key_start(c), size), :], q, (((1,), (1,)), ((), ())),
                                             preferred_element_type=F32)

    def softmax(slot, m, size=tk, n_valid=None):
        s = s_buf[slot, :size]
        if n_valid is not None:
            s = jnp.where(lax.broadcasted_iota(jnp.int32, s.shape, 0) < n_valid, s, NEG_BIG)
        m_new = jnp.maximum(m, jnp.max(s, axis=0, keepdims=True))
        p_buf[slot, :size] = jnp.exp2(s - m_new).astype(BF16)
        return m_new, jnp.exp2(m - m_new)

    def pv(c, slot, alpha, size=tk):
        acc_ref[...] = alpha * acc_ref[...] + jnp.dot(vt_buf[:, pl.ds(key_start(c), size)], p_buf[slot, :size],
                                                      preferred_element_type=F32)

    def step(c, par, m, alpha, next_size=tk):
        pv(c, par, alpha)
        m, alpha = softmax(1 - par, m)
        qk(c + 2, par, next_size)
        return m, alpha

    def pair(i, carry):
        m, alpha = step(2 * i, 0, *carry)
        return step(2 * i + 1, 1, m, alpha)

    n = n_chunks
    last = -(-last_valid // LANE) * LANE
    qk(0, 0)
    qk(1, 1)
    m, alpha = softmax(0, jnp.full((1, tq), NEG_BIG, F32))
    m, alpha = lax.fori_loop(0, (n - 3) // 2, pair, (m, alpha))
    if (n - 3) % 2:
        m, alpha = step(n - 4, 0, m, alpha)
    m, alpha = step(n - 3, (n - 3) % 2, m, alpha, next_size=last)
    par = (n - 2) % 2
    pv(n - 2, par, alpha)
    m, alpha = softmax(1 - par, m, last, last_valid)
    pv(n - 1, 1 - par, alpha, last)
    acc = acc_ref[...]
    o = acc[:V_HEAD] / acc[V_HEAD:V_HEAD + 1]
    o_ref[...] = o.T.astype(o_ref.dtype)


def flash_attention(q, k, vt, o_prev, row0, n_rows, n_valid, tq=512, tk=512):
    H, T, _ = q.shape
    q0 = row0 // tq
    n_chunks = -(-n_valid // tk)
    last_valid = n_valid - (n_chunks - 1) * tk
    assert n_chunks >= 3 and n_chunks * tk <= n_rows
    return pl.pallas_call(
        functools.partial(_flash_body, row0=row0, n_chunks=n_chunks, last_valid=last_valid, tk=tk),
        grid=(H, n_rows // tq),
        in_specs=[pl.BlockSpec((1, tq, QK_HEAD), lambda h, i: (h, q0 + i, 0)),
                  pl.BlockSpec(memory_space=pl.ANY),
                  pl.BlockSpec(memory_space=pl.ANY),
                  pl.BlockSpec(memory_space=pl.ANY)],
        out_specs=pl.BlockSpec((tq, V_HEAD), lambda h, i: (q0 + i, h)),
        out_shape=jax.ShapeDtypeStruct((T, H * V_HEAD), BF16),
        scratch_shapes=[pltpu.VMEM((n_rows, QK_HEAD), BF16), pltpu.VMEM((VT_ROWS, n_rows), BF16),
                        pltpu.VMEM((VT_ROWS, tq), F32), pltpu.VMEM((2, tk, tq), F32), pltpu.VMEM((2, tk, tq), BF16),
                        pltpu.SemaphoreType.DMA((2,))],
        input_output_aliases={3: 0},
        compiler_params=_cparams("arbitrary", "arbitrary"),
        name="flash_attention",
    )(q, k, vt, o_prev)


META_E1, META_E2, META_G1, META_G2, META_R1, META_R2 = range(6)


def _router_body(h_ref, g_ref, wr_ref, xn_ref, meta_ref, cnt_ref, carry_ref, *, token_ranges):
    i = pl.program_id(0)

    @pl.when(i == 0)
    def _():
        carry_ref[...] = jnp.zeros_like(carry_ref)

    xn = _rms(h_ref[...], g_ref[...])
    xn_ref[...] = xn
    logits = jnp.dot(xn, wr_ref[...], precision=lax.Precision.HIGHEST, preferred_element_type=F32)
    tb = logits.shape[0]
    lane = lax.broadcasted_iota(jnp.int32, logits.shape, 1)
    lg = jnp.where(lane < N_EXPERTS, logits, NEG_BIG)
    m1 = jnp.max(lg, axis=-1, keepdims=True)
    i1 = jnp.min(jnp.where(lg == m1, lane, LANE), axis=-1, keepdims=True)
    lg2 = jnp.where(lane == i1, NEG_BIG, lg)
    m2 = jnp.max(lg2, axis=-1, keepdims=True)
    i2 = jnp.min(jnp.where(lg2 == m2, lane, LANE), axis=-1, keepdims=True)
    e = jnp.exp(m2 - m1)
    row = lax.broadcasted_iota(jnp.int32, (tb, 1), 0) + i * tb
    valid = functools.reduce(jnp.logical_or, [(row >= a) & (row < b) for a, b in token_ranges])
    g1 = jnp.where(valid, 1.0 / (1.0 + e), 0.0)
    g2 = jnp.where(valid, e / (1.0 + e), 0.0)
    oh = jnp.where(valid & ((lane == i1) | (lane == i2)), 1.0, 0.0)
    r = lax.broadcasted_iota(jnp.int32, (tb, tb), 0)
    c = lax.broadcasted_iota(jnp.int32, (tb, tb), 1)
    tri = jnp.where(c < r, 1.0, 0.0).astype(BF16)
    pre = jnp.dot(tri, oh.astype(BF16), preferred_element_type=F32) + carry_ref[0:1, :]
    r1 = jnp.sum(jnp.where(lane == i1, pre, 0.0), axis=-1, keepdims=True)
    r2 = jnp.sum(jnp.where(lane == i2, pre, 0.0), axis=-1, keepdims=True)
    carry_ref[...] += jnp.broadcast_to(jnp.sum(oh, axis=0, keepdims=True), carry_ref.shape)
    meta = jnp.zeros(logits.shape, F32)
    for idx, val in ((META_E1, i1.astype(F32)), (META_E2, i2.astype(F32)), (META_G1, g1), (META_G2, g2),
                     (META_R1, r1), (META_R2, r2)):
        meta = jnp.where(lane == idx, val, meta)
    meta_ref[...] = meta
    cnt_ref[...] = carry_ref[...]


def moe_router(h, g, w_router_pad, token_ranges, tb=256):
    T, D = h.shape
    return pl.pallas_call(
        functools.partial(_router_body, token_ranges=token_ranges),
        grid=(T // tb,),
        in_specs=[pl.BlockSpec((tb, D), lambda i: (i, 0)),
                  pl.BlockSpec((1, D), lambda i: (0, 0)),
                  pl.BlockSpec((D, LANE), lambda i: (0, 0))],
        out_specs=[pl.BlockSpec((tb, D), lambda i: (i, 0)),
                   pl.BlockSpec((tb, LANE), lambda i: (i, 0)),
                   pl.BlockSpec((SUBLANE, LANE), lambda i: (0, 0))],
        out_shape=[jax.ShapeDtypeStruct((T, D), F32),
                   jax.ShapeDtypeStruct((T, LANE), F32),
                   jax.ShapeDtypeStruct((SUBLANE, LANE), F32)],
        scratch_shapes=[pltpu.VMEM((SUBLANE, LANE), F32)],
        compiler_params=_cparams("arbitrary"),
        name="moe_router",
    )(h, g.reshape(1, D), w_router_pad)


def _expert_up_body(te_ref, nu_ref, src_ref, xn_hbm, wg_ref, wu_ref, o_ref, xf_ref, xb_ref, sem):
    active = pl.program_id(0) < nu_ref[0]
    tm = xf_ref.shape[0]

    @pl.when(active & (pl.program_id(1) == 0))
    def _():
        def row_copy(r):
            return pltpu.make_async_copy(xn_hbm.at[pl.ds(src_ref[0, 0, r], 1)], xf_ref.at[pl.ds(r, 1)], sem)

        def start(r, c):
            row_copy(r).start()
            return c

        def wait(r, c):
            row_copy(r).wait()
            return c

        lax.fori_loop(0, tm, start, 0, unroll=DMA_UNROLL)
        lax.fori_loop(0, tm, wait, 0, unroll=DMA_UNROLL)
        xb_ref[...] = xf_ref[...].astype(BF16)

    @pl.when(active)
    def _():
        x = xb_ref[...]
        a = jnp.dot(x, wg_ref[0], preferred_element_type=F32)
        b = jnp.dot(x, wu_ref[0], preferred_element_type=F32)
        o_ref[...] = _silu_mul(a, b).astype(o_ref.dtype)

    @pl.when(jnp.logical_not(active))
    def _():
        o_ref[...] = jnp.zeros_like(o_ref)


def expert_up(xn, src, wg, wu, tile_expert, n_used, tm, tf=2048):
    D = xn.shape[1]
    n_tiles = src.shape[0] // tm
    F = wg.shape[2]
    tf = _tile(F, tf)
    grid_spec = pltpu.PrefetchScalarGridSpec(
        num_scalar_prefetch=2,
        grid=(n_tiles, F // tf),
        in_specs=[pl.BlockSpec((1, 1, tm), lambda i, j, te, nu: (i, 0, 0), memory_space=pltpu.SMEM),
                  pl.BlockSpec(memory_space=pl.ANY),
                  pl.BlockSpec((1, D, tf), lambda i, j, te, nu: (te[i], 0, j)),
                  pl.BlockSpec((1, D, tf), lambda i, j, te, nu: (te[i], 0, j))],
        out_specs=pl.BlockSpec((tm, tf), lambda i, j, te, nu: (i, j)),
        scratch_shapes=[pltpu.VMEM((tm, D), F32), pltpu.VMEM((tm, D), BF16), pltpu.SemaphoreType.DMA(())],
    )
    return pl.pallas_call(
        _expert_up_body,
        grid_spec=grid_spec,
        out_shape=jax.ShapeDtypeStruct((n_tiles * tm, F), BF16),
        compiler_params=_cparams("arbitrary", "arbitrary"),
        name="moe_expert_up",
    )(tile_expert, n_used, src.reshape(n_tiles, 1, tm), xn, wg, wu)


def _expert_down_body(te_ref, nu_ref, a_ref, w_ref, o_ref):
    @pl.when(pl.program_id(0) < nu_ref[0])
    def _():
        o_ref[...] = jnp.dot(a_ref[...], w_ref[0], preferred_element_type=F32)

    @pl.when(pl.program_id(0) >= nu_ref[0])
    def _():
        o_ref[...] = jnp.zeros_like(o_ref)


def expert_down(act, wd, tile_expert, n_used, tm, tn=512):
    S, F = act.shape
    D = wd.shape[2]
    tn = _tile(D, tn)
    grid_spec = pltpu.PrefetchScalarGridSpec(
        num_scalar_prefetch=2,
        grid=(S // tm, D // tn),
        in_specs=[pl.BlockSpec((tm, F), lambda i, j, te, nu: (i, 0)),
                  pl.BlockSpec((1, F, tn), lambda i, j, te, nu: (te[i], 0, j))],
        out_specs=pl.BlockSpec((tm, tn), lambda i, j, te, nu: (i, j)),
    )
    return pl.pallas_call(
        _expert_down_body,
        grid_spec=grid_spec,
        out_shape=jax.ShapeDtypeStruct((S, D), F32),
        compiler_params=_cparams("parallel", "parallel"),
        name="moe_expert_down",
    )(tile_expert, n_used, act, wd)


def _combine_body(pos_ref, h_ref, meta_ref, ys_hbm, o_ref, buf, sem, *, tb):
    def row_copy(r, k):
        return pltpu.make_async_copy(ys_hbm.at[pl.ds(pos_ref[0, 0, 2 * r + k], 1)],
                                     buf.at[k, pl.ds(r, 1)], sem)

    def start(r, c):
        row_copy(r, 0).start()
        row_copy(r, 1).start()
        return c

    def wait(r, c):
        row_copy(r, 0).wait()
        row_copy(r, 1).wait()
        return c

    lax.fori_loop(0, tb, start, 0, unroll=DMA_UNROLL)
    lax.fori_loop(0, tb, wait, 0, unroll=DMA_UNROLL)
    g1 = meta_ref[:, META_G1:META_G1 + 1]
    g2 = meta_ref[:, META_G2:META_G2 + 1]
    o_ref[...] = h_ref[...] + g1 * buf[0] + g2 * buf[1]


def moe_combine(h, meta, pos, ys, tb=256):
    T, D = h.shape
    pos3 = pos.reshape(T // tb, 1, 2 * tb)
    return pl.pallas_call(
        functools.partial(_combine_body, tb=tb),
        grid=(T // tb,),
        in_specs=[pl.BlockSpec((1, 1, 2 * tb), lambda i: (i, 0, 0), memory_space=pltpu.SMEM),
                  pl.BlockSpec((tb, D), lambda i: (i, 0)),
                  pl.BlockSpec((tb, LANE), lambda i: (i, 0)),
                  pl.BlockSpec(memory_space=pl.ANY)],
        out_specs=pl.BlockSpec((tb, D), lambda i: (i, 0)),
        out_shape=jax.ShapeDtypeStruct((T, D), F32),
        scratch_shapes=[pltpu.VMEM((2, tb, D), F32), pltpu.SemaphoreType.DMA(())],
        compiler_params=_cparams("arbitrary"),
        name="moe_combine",
    )(pos3, h, meta, ys)


def moe_layer(h, g, w_router, wg, wu, wd, token_ranges, tm=512):
    T, D = h.shape
    wr = jnp.pad(w_router, ((0, 0), (0, LANE - N_EXPERTS)))
    xn, meta, cnt = moe_router(h, g, wr, token_ranges)
    counts = cnt[0, :N_EXPERTS].astype(jnp.int32)
    tiles = (counts + tm - 1) // tm
    tile_end = jnp.cumsum(tiles)
    starts = (tile_end - tiles) * tm
    n_tok = sum(b - a for a, b in token_ranges)
    n_tiles = (2 * n_tok + N_EXPERTS * (tm - 1)) // tm
    e_idx = meta[:, META_E1:META_E2 + 1].astype(jnp.int32)
    rank = meta[:, META_R1:META_R2 + 1].astype(jnp.int32)
    row = jnp.arange(T, dtype=jnp.int32)[:, None]
    valid = functools.reduce(jnp.logical_or, [(row >= a) & (row < b) for a, b in token_ranges])
    pos = starts[e_idx] + rank
    tile_expert = jnp.minimum(jnp.sum(jnp.arange(n_tiles)[:, None] >= tile_end[None, :], axis=1),
                              N_EXPERTS - 1).astype(jnp.int32)
    n_used = tile_end[-1:].astype(jnp.int32)
    tok = jnp.repeat(row[:, 0], 2)
    src = jnp.zeros((n_tiles * tm,), jnp.int32).at[jnp.where(valid, pos, n_tiles * tm).reshape(-1)].set(
        tok, mode="drop")
    act = expert_up(xn, src, wg, wu, tile_expert, n_used, tm)
    ys = expert_down(act, wd, tile_expert, n_used, tm)
    return moe_combine(h, meta, jnp.where(valid, pos, 0), ys)


def _padded_len(L):
    r = -(-L // FFT_C)
    r_align = TOK_TILE // FFT_C
    r = -(-r // r_align) * r_align
    return r * FFT_C


def _rope_tables(lens, pads):
    half = QK_ROPE // 2
    inv = 1.0 / (ROPE_BASE ** (jnp.arange(0, QK_ROPE, 2, dtype=F32) / QK_ROPE))
    cs, ss = [], []
    for L, P in zip(lens, pads):
        ang = jnp.arange(P, dtype=F32)[:, None] * inv[None, :]
        c, s = jnp.cos(ang), jnp.sin(ang)
        cs.append(jnp.concatenate([jnp.ones((P, QK_NOPE), F32), c, c], axis=-1))
        ss.append(jnp.concatenate([jnp.zeros((P, QK_NOPE), F32), s, s], axis=-1))
    rot = np.zeros((QK_HEAD, QK_HEAD), np.float32)
    for c in range(QK_NOPE, QK_NOPE + half):
        rot[c + half, c] = -1.0
        rot[c, c + half] = 1.0
    return jnp.concatenate(cs), jnp.concatenate(ss), jnp.asarray(rot, BF16)


def kernel(x_prompt, x_sample, meta_tokens, norm_mix, norm_ffn, hy_w_in, hy_conv_w, hy_conv_b, hy_filt_w1, hy_filt_b1, hy_filt_w2, hy_filt_b2, hy_filt_w3, hy_filt_freq, hy_filt_log_decay, hy_skip, hy_w_out, ffn_w_gate, ffn_w_up, ffn_w_down, mla_w_dq, mla_g_q, mla_w_uq, mla_w_dkv, mla_g_kv, mla_w_ukv, mla_g_qhead, mla_g_khead, mla_w_o, moe_w_router, moe_w_gate, moe_w_up, moe_w_down):
    D = x_prompt.shape[-1]
    n_meta = meta_tokens.shape[0]
    assert x_prompt.shape[0] == 1 and x_sample.shape[0] == 1
    groups = (x_prompt[0], x_sample[0])
    lens = [n_meta + g.shape[0] for g in groups]
    pads = [_padded_len(L) for L in lens]
    row0s = [0, pads[0]]
    T = sum(pads)
    assert T % TOK_TILE == 0 and all(p % TOK_TILE == 0 for p in pads)

    parts = []
    for g, L, P in zip(groups, lens, pads):
        parts += [meta_tokens.astype(F32), g, jnp.zeros((P - L, D), F32)]
    h = jnp.concatenate(parts, axis=0)

    bf = lambda w: w.astype(BF16)

    u = norm_mm(h, norm_mix[0], [bf(hy_w_in)], lambda y: y, BF16, 2048, "hyena_in_proj")
    n_orders = hy_skip.shape[0]
    fh = hy_filt_w2.shape[0]
    filt = (jnp.pad(hy_filt_w1, ((0, LANE - POS_EMB_DIM), (0, 0))), hy_filt_b1, hy_filt_w2, hy_filt_b2,
            bf(hy_filt_w3.reshape(fh, n_orders * 2 * D)), hy_filt_freq, hy_filt_log_decay.reshape(-1))
    zs = []
    for L, P, r0 in zip(lens, pads, row0s):
        uc = short_conv(u, hy_conv_w, hy_conv_b, r0, P, L)
        zs.append(hyena_group(uc, L, D, filt, hy_skip))
    z = jnp.concatenate(zs, axis=0)
    h = mm_res(z, bf(hy_w_out), h, 2048, "hyena_out_proj")

    a = norm_mm(h, norm_ffn[0], [bf(ffn_w_gate), bf(ffn_w_up)], _silu_mul, BF16, 1024, "ffn_swiglu")
    h = mm_res(a, bf(ffn_w_down), h, 512, "ffn_down")

    q_lora = mla_w_dq.shape[1]
    kv_lora = mla_w_ukv.shape[0]
    cw = q_lora + mla_w_dkv.shape[1]
    cw_pad = -(-cw // LANE) * LANE
    w_down = jnp.pad(jnp.concatenate([mla_w_dq, mla_w_dkv], axis=1), ((0, 0), (0, cw_pad - cw)))
    cqkv = norm_mm(h, norm_mix[1], [bf(w_down)], lambda y: y, F32, cw_pad, "mla_down_proj")
    cosf, sinf, rot = _rope_tables(lens, pads)
    w_uq_h = bf(mla_w_uq.reshape(q_lora, MLA_HEADS, QK_HEAD).transpose(1, 0, 2))
    w_ukv_h = mla_w_ukv.reshape(kv_lora, MLA_HEADS, QK_NOPE + V_HEAD).transpose(1, 0, 2)
    w_uk_h = bf(w_ukv_h[:, :, :QK_NOPE])
    w_uvt_h = bf(w_ukv_h[:, :, QK_NOPE:].transpose(0, 2, 1))
    q = mla_q_heads(cqkv, mla_g_q, w_uq_h, mla_g_qhead, rot, cosf, sinf, q_lora)
    k, vt = mla_kv_heads(cqkv, mla_g_kv, w_uk_h, w_uvt_h, mla_g_khead, rot, cosf, sinf, q_lora, kv_lora)
    o = jnp.zeros((T, MLA_HEADS * V_HEAD), BF16)
    for L, P, r0 in zip(lens, pads, row0s):
        o = flash_attention(q, k, vt, o, r0, P, L)
    h = mm_res(o, bf(mla_w_o), h, 2048, "mla_out_proj")

    token_ranges = tuple((r0, r0 + L) for r0, L in zip(row0s, lens))
    h = moe_layer(h, norm_ffn[1], moe_w_router, bf(moe_w_gate), bf(moe_w_up), bf(moe_w_down), token_ranges)

    outs = []
    for g, r0 in zip(groups, row0s):
        outs.append(h[r0 + n_meta:r0 + n_meta + g.shape[0]][None])
    return tuple(outs)
```

```python
import functools
import math

import numpy as np
import jax
import jax.numpy as jnp
from jax import lax
from jax.experimental import pallas as pl
from jax.experimental.pallas import tpu as pltpu

F32 = jnp.float32
BF16 = jnp.bfloat16

EPS = 1e-6
MLA_HEADS = 16
QK_NOPE = 128
QK_ROPE = 64
QK_HEAD = QK_NOPE + QK_ROPE
V_HEAD = 128
VT_ROWS = V_HEAD + 16
ROPE_BASE = 10000.0
N_EXPERTS = 8
POS_EMB_DIM = 33

LANE = 128
SUBLANE = 8
VMEM_LIMIT_BYTES = 56 * 1024 * 1024

FFT_C = 128
FFT_JB = 2 * SUBLANE
FFT_KB = SUBLANE

TOK_TILE = 512
DMA_UNROLL = 8
NEG_BIG = -1e30


def _cparams(*sem):
    return pltpu.CompilerParams(dimension_semantics=sem, vmem_limit_bytes=VMEM_LIMIT_BYTES)


def _tile(n, pref):
    if n <= pref:
        return n
    t = pref - pref % LANE
    while n % t:
        t -= LANE
    return t


def _rms(x, g):
    ms = jnp.mean(x * x, axis=-1, keepdims=True)
    return x * lax.rsqrt(ms + EPS) * g


def _norm_mm_body(x_ref, g_ref, *rest, n_w, epilogue):
    w_refs, o_ref, xn_ref = rest[:n_w], rest[n_w], rest[n_w + 1]

    @pl.when(pl.program_id(1) == 0)
    def _():
        xn_ref[...] = _rms(x_ref[...], g_ref[...]).astype(BF16)

    xn = xn_ref[...]
    ys = [jnp.dot(xn, w[...], preferred_element_type=F32) for w in w_refs]
    o_ref[...] = epilogue(*ys).astype(o_ref.dtype)


def norm_mm(x, g, ws, epilogue, out_dtype, tn, name, tm=TOK_TILE):
    T, K = x.shape
    N = ws[0].shape[1]
    tn = _tile(N, tn)
    n_w = len(ws)
    return pl.pallas_call(
        functools.partial(_norm_mm_body, n_w=n_w, epilogue=epilogue),
        grid=(T // tm, N // tn),
        in_specs=[pl.BlockSpec((tm, K), lambda i, j: (i, 0)),
                  pl.BlockSpec((1, K), lambda i, j: (0, 0))]
        + [pl.BlockSpec((K, tn), lambda i, j: (0, j))] * n_w,
        out_specs=pl.BlockSpec((tm, tn), lambda i, j: (i, j)),
        out_shape=jax.ShapeDtypeStruct((T, N), out_dtype),
        scratch_shapes=[pltpu.VMEM((tm, K), BF16)],
        compiler_params=_cparams("parallel", "arbitrary"),
        name=name,
    )(x, g.reshape(1, K), *ws)


def _mm_res_body(x_ref, w_ref, r_ref, o_ref):
    o_ref[...] = r_ref[...] + jnp.dot(x_ref[...].astype(BF16), w_ref[...], preferred_element_type=F32)


def mm_res(x, w, res, tn, name, tm=TOK_TILE):
    T, K = x.shape
    N = w.shape[1]
    tn = _tile(N, tn)
    return pl.pallas_call(
        _mm_res_body,
        grid=(T // tm, N // tn),
        in_specs=[pl.BlockSpec((tm, K), lambda i, j: (i, 0)),
                  pl.BlockSpec((K, tn), lambda i, j: (0, j)),
                  pl.BlockSpec((tm, tn), lambda i, j: (i, j))],
        out_specs=pl.BlockSpec((tm, tn), lambda i, j: (i, j)),
        out_shape=jax.ShapeDtypeStruct((T, N), F32),
        compiler_params=_cparams("parallel", "parallel"),
        name=name,
    )(x, w, res)


def _silu_mul(a, b):
    return a * (1.0 / (1.0 + jnp.exp(-a))) * b


def _short_conv_body(u_ref, prev_ref, next_ref, w_ref, b_ref, o_ref, *, row0, n_valid, n_rows_total, halo):
    i = pl.program_id(0)
    tm = u_ref.shape[0]
    u = u_ref[...].astype(F32)
    row = lax.broadcasted_iota(jnp.int32, u.shape, 0)
    g_row = row0 + i * tm
    prev_row = jnp.where(g_row == 0, 0.0, prev_ref[...].astype(F32)[halo - 1:halo, :])
    next_row = jnp.where(g_row + tm == n_rows_total, 0.0, next_ref[...].astype(F32)[0:1, :])
    up = jnp.where(row == 0, prev_row, pltpu.roll(u, 1, 0))
    dn = jnp.where(row == tm - 1, next_row, pltpu.roll(u, tm - 1, 0))
    y = b_ref[...] + up * w_ref[0:1, :] + u * w_ref[1:2, :] + dn * w_ref[2:3, :]
    o_ref[...] = jnp.where(i * tm + row < n_valid, y, 0.0)


def short_conv(u, w, b, row0, n_rows, n_valid, tm=TOK_TILE, tn=1024):
    T, C3 = u.shape
    tn = _tile(C3, tn)
    halo = 16
    r0 = row0 // tm
    hb = tm // halo
    nhalo = T // halo
    return pl.pallas_call(
        functools.partial(_short_conv_body, row0=row0, n_valid=n_valid, n_rows_total=T, halo=halo),
        grid=(n_rows // tm, C3 // tn),
        in_specs=[pl.BlockSpec((tm, tn), lambda i, j: (r0 + i, j)),
                  pl.BlockSpec((halo, tn), lambda i, j: (jnp.maximum((r0 + i) * hb - 1, 0), j)),
                  pl.BlockSpec((halo, tn), lambda i, j: (jnp.minimum((r0 + i + 1) * hb, nhalo - 1), j)),
                  pl.BlockSpec((3, tn), lambda i, j: (0, j)),
                  pl.BlockSpec((1, tn), lambda i, j: (0, j))],
        out_specs=pl.BlockSpec((tm, tn), lambda i, j: (i, j)),
        out_shape=jax.ShapeDtypeStruct((n_rows, C3), F32),
        compiler_params=_cparams("parallel", "parallel"),
        name="hyena_short_conv",
    )(u, u, u, w, b.reshape(1, C3))


def _filter_body(z_ref, w1_ref, b1_ref, w2_ref, b2_ref, w3_ref, fr_ref, dec_ref, t_ref,
                 h_ref, ssq_ref, hdn_ref, *, n_valid):
    i = pl.program_id(0)
    tm = z_ref.shape[0]

    @pl.when(pl.program_id(1) == 0)
    def _():
        hi = lax.Precision.HIGHEST
        a = jnp.dot(z_ref[...], w1_ref[...], precision=hi, preferred_element_type=F32) + b1_ref[...]
        a = jnp.sin(fr_ref[0:1, :] * a)
        a = jnp.dot(a, w2_ref[...], precision=hi, preferred_element_type=F32) + b2_ref[...]
        hdn_ref[...] = jnp.sin(fr_ref[1:2, :] * a).astype(BF16)

    h = jnp.dot(hdn_ref[...], w3_ref[...], preferred_element_type=F32)
    window = jnp.exp(-t_ref[...] * jnp.exp(dec_ref[...]))
    row = lax.broadcasted_iota(jnp.int32, h.shape, 0) + i * tm
    h = jnp.where(row < n_valid, h * window, 0.0)
    h_ref[...] = h
    ssq_ref[...] = jnp.broadcast_to(jnp.sum(h * h, axis=0, keepdims=True), ssq_ref.shape)


def hyena_filter(zfeat, tcol, w1, b1, w2, b2, w3, freq, log_decay, n_valid, tm=512, tc=2048):
    P = zfeat.shape[0]
    FH, NC = w3.shape
    tc = _tile(NC, tc)
    return pl.pallas_call(
        functools.partial(_filter_body, n_valid=n_valid),
        grid=(P // tm, NC // tc),
        in_specs=[pl.BlockSpec((tm, LANE), lambda i, c: (i, 0)),
                  pl.BlockSpec((LANE, FH), lambda i, c: (0, 0)),
                  pl.BlockSpec((1, FH), lambda i, c: (0, 0)),
                  pl.BlockSpec((FH, FH), lambda i, c: (0, 0)),
                  pl.BlockSpec((1, FH), lambda i, c: (0, 0)),
                  pl.BlockSpec((FH, tc), lambda i, c: (0, c)),
                  pl.BlockSpec((2, FH), lambda i, c: (0, 0)),
                  pl.BlockSpec((1, tc), lambda i, c: (0, c)),
                  pl.BlockSpec((tm, 1), lambda i, c: (i, 0))],
        out_specs=[pl.BlockSpec((tm, tc), lambda i, c: (i, c)),
                   pl.BlockSpec((SUBLANE, tc), lambda i, c: (i, c))],
        out_shape=[jax.ShapeDtypeStruct((P, NC), F32),
                   jax.ShapeDtypeStruct((P // tm * SUBLANE, NC), F32)],
        scratch_shapes=[pltpu.VMEM((tm, FH), BF16)],
        compiler_params=_cparams("parallel", "arbitrary"),
        name="hyena_filter",
    )(zfeat, w1, b1.reshape(1, FH), w2, b2.reshape(1, FH), w3, freq, log_decay.reshape(1, NC), tcol)


def _fft_fwd_body(x_ref, g_ref, y_ref):
    xt = jnp.swapaxes(x_ref[...], 0, 1).astype(BF16)
    ys = [jnp.dot(g_ref[j], xt[j], preferred_element_type=F32) for j in range(FFT_JB)]
    y = jnp.swapaxes(jnp.stack(ys), 0, 1)
    y_ref[...] = y.reshape(y_ref.shape).astype(y_ref.dtype)


def fft_fwd(x3, col0, ncols, gfwd, td=256):
    R, C, _ = x3.shape
    n1 = gfwd.shape[1] // 2
    td = _tile(ncols, td)
    c0 = col0 // td
    return pl.pallas_call(
        _fft_fwd_body,
        grid=(C // FFT_JB, ncols // td),
        in_specs=[pl.BlockSpec((R, FFT_JB, td), lambda i, j: (0, i, c0 + j)),
                  pl.BlockSpec((FFT_JB, 2 * n1, R), lambda i, j: (i, 0, 0))],
        out_specs=pl.BlockSpec((2, n1, FFT_JB, td), lambda i, j: (0, 0, i, j)),
        out_shape=jax.ShapeDtypeStruct((2, n1, C, ncols), BF16),
        compiler_params=_cparams("parallel", "parallel"),
        name="fft_fwd",
    )(x3, gfwd)


def _spectral_body(y_ref, ya_ref, yb_ref, s_ref, wf_ref, wi_ref, o_ref):
    c, td = y_ref.shape[2], y_ref.shape[3]
    s = s_ref[...]
    for kk in range(FFT_KB):
        x = jnp.dot(wf_ref[...], y_ref[:, kk].reshape(2 * c, td), preferred_element_type=F32)
        sa = jnp.dot(wf_ref[...], ya_ref[:, kk].reshape(2 * c, td), preferred_element_type=F32)
        sb = jnp.dot(wf_ref[...], yb_ref[:, kk].reshape(2 * c, td), preferred_element_type=F32)
        xr, xi = x[:c], x[c:]
        kr, ki = (sa[:c] + sb[:c]) * s, (sa[c:] - sb[c:]) * s
        z = jnp.concatenate([xr * kr - xi * ki, xr * ki + xi * kr], axis=0).astype(BF16)
        o = jnp.dot(wi_ref[...], z, preferred_element_type=F32)
        o_ref[:, kk] = o.reshape(2, c, td).astype(o_ref.dtype)


def spectral_mul(y, yh, order, scale, wf, wi, td=1024):
    _, n1, C, D = y.shape
    td = _tile(D, td)
    ca = (order * 2 * D) // td
    cb = (order * 2 * D + D) // td
    mat = pl.BlockSpec((2 * C, 2 * C), lambda a, j: (0, 0))
    return pl.pallas_call(
        _spectral_body,
        grid=(n1 // FFT_KB, D // td),
        in_specs=[pl.BlockSpec((2, FFT_KB, C, td), lambda a, j: (0, a, 0, j)),
                  pl.BlockSpec((2, FFT_KB, C, td), lambda a, j: (0, a, 0, ca + j)),
                  pl.BlockSpec((2, FFT_KB, C, td), lambda a, j: (0, a, 0, cb + j)),
                  pl.BlockSpec((1, td), lambda a, j: (0, j)),
                  mat, mat],
        out_specs=pl.BlockSpec((2, FFT_KB, C, td), lambda a, j: (0, a, 0, j)),
        out_shape=jax.ShapeDtypeStruct((2, n1, C, D), BF16),
        compiler_params=_cparams("parallel", "parallel"),
        name="spectral_mul",
    )(y, yh, yh, scale, wf, wi)


def _fft_inv_body(y_ref, gi_ref, z_ref, gate_ref, skip_ref, o_ref, *, n_valid):
    i = pl.program_id(0)
    n1c2, jb, td = 2 * y_ref.shape[1], y_ref.shape[2], y_ref.shape[3]
    yt = jnp.swapaxes(y_ref[...].reshape(n1c2, jb, td).astype(F32), 0, 1).astype(BF16)
    convs = [jnp.dot(gi_ref[j], yt[j], preferred_element_type=F32) for j in range(jb)]
    conv = jnp.swapaxes(jnp.stack(convs), 0, 1)
    z = gate_ref[...] * (conv + z_ref[...] * skip_ref[...])
    seq = (lax.broadcasted_iota(jnp.int32, z.shape, 0) * FFT_C
           + lax.broadcasted_iota(jnp.int32, z.shape, 1) + i * jb)
    o_ref[...] = jnp.where(seq < n_valid, z, 0.0)


def fft_inv_gate(y, ginv, z3, zcol0, gate3, gcol0, skip, n_valid, td=256):
    _, n1, C, D = y.shape
    R = ginv.shape[1]
    td = _tile(D, td)
    zc, gc = zcol0 // td, gcol0 // td
    return pl.pallas_call(
        functools.partial(_fft_inv_body, n_valid=n_valid),
        grid=(C // FFT_JB, D // td),
        in_specs=[pl.BlockSpec((2, n1, FFT_JB, td), lambda i, j: (0, 0, i, j)),
                  pl.BlockSpec((FFT_JB, R, 2 * n1), lambda i, j: (i, 0, 0)),
                  pl.BlockSpec((R, FFT_JB, td), lambda i, j: (0, i, zc + j)),
                  pl.BlockSpec((R, FFT_JB, td), lambda i, j: (0, i, gc + j)),
                  pl.BlockSpec((1, 1, td), lambda i, j: (0, 0, j))],
        out_specs=pl.BlockSpec((R, FFT_JB, td), lambda i, j: (0, i, j)),
        out_shape=jax.ShapeDtypeStruct((R, C, D), F32),
        compiler_params=_cparams("parallel", "parallel"),
        name="fft_inv_gate",
    )(y, ginv, z3, gate3, skip.reshape(1, 1, -1))


def _dft_tables(R):
    C = FFT_C
    N = 2 * R * C
    n1_pad = 2 * FFT_KB
    n1c = -(-(R + 1) // n1_pad) * n1_pad
    k1 = jnp.arange(n1c, dtype=jnp.int32)[None, :, None]
    n1 = jnp.arange(R, dtype=jnp.int32)[None, None, :]
    n2 = jnp.arange(C, dtype=jnp.int32)[:, None, None]
    ang = (2.0 * math.pi / N) * ((k1 * (C * n1 + n2)) % N).astype(F32)
    used = (k1 <= R).astype(F32)
    mirror = jnp.where((k1 == 0) | (k1 == R), 1.0, 2.0) * used
    gr, gim = jnp.cos(ang), -jnp.sin(ang)
    gfwd = jnp.concatenate([gr * used, gim * used], axis=1).astype(BF16)
    ginv = jnp.concatenate([gr * mirror, gim * mirror], axis=1).transpose(0, 2, 1).astype(BF16)
    a = jnp.arange(C, dtype=jnp.int32)
    angc = (2.0 * math.pi / C) * ((a[:, None] * a[None, :]) % C).astype(F32)
    fr, fi = jnp.cos(angc), -jnp.sin(angc)
    wf = jnp.block([[fr, -fi], [fi, fr]]).astype(BF16)
    wi = jnp.block([[fr, fi], [-fi, fr]]).astype(BF16)
    return gfwd, ginv, wf, wi, N


def _pos_features(L, P):
    bands = (POS_EMB_DIM - 1) // 2
    i = jnp.arange(P, dtype=F32)
    t = jnp.linspace(0.0, 1.0, L, dtype=F32)
    t = jnp.concatenate([t, jnp.ones((P - L,), F32)])[:, None]
    w = (2.0 * math.pi / L) * i[:, None]
    f = jnp.linspace(1e-4, bands - 1, bands, dtype=F32)[None, :]
    z = jnp.concatenate([t, jnp.cos(f * w), -jnp.sin(f * w)], axis=-1)
    return jnp.pad(z, ((0, 0), (0, LANE - POS_EMB_DIM))), t


def hyena_group(uc, L, D, filt, skip):
    P = uc.shape[0]
    R = P // FFT_C
    w1p, b1, w2, b2, w3f, freq, dec = filt
    gfwd, ginv, wf, wi, N = _dft_tables(R)
    zfeat, tcol = _pos_features(L, P)
    hw, ssq = hyena_filter(zfeat, tcol, w1p, b1, w2, b2, w3f, freq, dec, L)
    n_orders = skip.shape[0]
    ssq = jnp.sum(ssq[::SUBLANE], axis=0).reshape(n_orders, 2, D)
    h0 = hw[0].reshape(n_orders, 2, D)
    total = ssq[:, 0] + ssq[:, 1] + 2.0 * h0[:, 0] * h0[:, 1]
    scale = lax.rsqrt(total + EPS) / N
    yh = fft_fwd(hw.reshape(R, FFT_C, hw.shape[1]), 0, hw.shape[1], gfwd)
    uc3 = uc.reshape(R, FFT_C, 3 * D)
    z3 = uc3
    for n in range(n_orders):
        y = fft_fwd(z3, 0, D, gfwd)
        y = spectral_mul(y, yh, n, scale[n:n + 1], wf, wi)
        z3 = fft_inv_gate(y, ginv, z3, 0, uc3, (n + 1) * D, skip[n:n + 1], L)
    return z3.reshape(P, D)


def _rope(x, rot_ref, cos_ref, sin_ref):
    xr = jnp.dot(x.astype(BF16), rot_ref[...], preferred_element_type=F32)
    return x * cos_ref[...] + xr * sin_ref[...]


def _q_head_body(c_ref, gq_ref, w_ref, gh_ref, rot_ref, cos_ref, sin_ref, q_ref, cn_ref, *, q_lora, scale):
    @pl.when(pl.program_id(1) == 0)
    def _():
        cn_ref[...] = _rms(c_ref[:, :q_lora], gq_ref[...]).astype(BF16)

    q = jnp.dot(cn_ref[...], w_ref[0], preferred_element_type=F32)
    q = _rope(_rms(q, gh_ref[...]), rot_ref, cos_ref, sin_ref)
    q_ref[0] = (q * scale).astype(q_ref.dtype)


def _head_tile(T):
    return 2 * TOK_TILE if T % (2 * TOK_TILE) == 0 else TOK_TILE


def mla_q_heads(cqkv, g_q, w_uq_h, g_qhead, rot, cosf, sinf, q_lora):
    T, CW = cqkv.shape
    tm = _head_tile(T)
    H = w_uq_h.shape[0]
    return pl.pallas_call(
        functools.partial(_q_head_body, q_lora=q_lora, scale=QK_HEAD ** -0.5 * math.log2(math.e)),
        grid=(T // tm, H),
        in_specs=[pl.BlockSpec((tm, CW), lambda i, h: (i, 0)),
                  pl.BlockSpec((1, q_lora), lambda i, h: (0, 0)),
                  pl.BlockSpec((1, q_lora, QK_HEAD), lambda i, h: (h, 0, 0)),
                  pl.BlockSpec((1, QK_HEAD), lambda i, h: (0, 0)),
                  pl.BlockSpec((QK_HEAD, QK_HEAD), lambda i, h: (0, 0)),
                  pl.BlockSpec((tm, QK_HEAD), lambda i, h: (i, 0)),
                  pl.BlockSpec((tm, QK_HEAD), lambda i, h: (i, 0))],
        out_specs=pl.BlockSpec((1, tm, QK_HEAD), lambda i, h: (h, i, 0)),
        out_shape=jax.ShapeDtypeStruct((H, T, QK_HEAD), BF16),
        scratch_shapes=[pltpu.VMEM((tm, q_lora), BF16)],
        compiler_params=_cparams("parallel", "arbitrary"),
        name="mla_q_heads",
    )(cqkv, g_q.reshape(1, -1), w_uq_h, g_qhead.reshape(1, -1), rot, cosf, sinf)


def _kv_head_body(c_ref, gkv_ref, wk_ref, wvt_ref, gh_ref, rot_ref, cos_ref, sin_ref, k_ref, vt_ref, cn_ref,
                  *, q_lora, kv_lora):
    @pl.when(pl.program_id(1) == 0)
    def _():
        cn_ref[...] = _rms(c_ref[:, q_lora:q_lora + kv_lora], gkv_ref[...]).astype(BF16)

    cn = cn_ref[...]
    k_nope = jnp.dot(cn, wk_ref[0], preferred_element_type=F32)
    k_pe = c_ref[:, q_lora + kv_lora:q_lora + kv_lora + QK_ROPE]
    k = jnp.concatenate([k_nope, k_pe], axis=-1)
    k = _rope(_rms(k, gh_ref[...]), rot_ref, cos_ref, sin_ref)
    k_ref[0] = k.astype(k_ref.dtype)
    vt = lax.dot_general(wvt_ref[0], cn, (((1,), (1,)), ((), ())), preferred_element_type=F32)
    vt_ref[0, :V_HEAD, :] = vt.astype(vt_ref.dtype)
    extra = vt_ref.shape[1] - V_HEAD
    row = lax.broadcasted_iota(jnp.int32, (extra, vt.shape[1]), 0)
    vt_ref[0, V_HEAD:, :] = jnp.where(row == 0, 1.0, 0.0).astype(vt_ref.dtype)


def mla_kv_heads(cqkv, g_kv, w_uk_h, w_uvt_h, g_khead, rot, cosf, sinf, q_lora, kv_lora):
    T, CW = cqkv.shape
    tm = _head_tile(T)
    H = w_uk_h.shape[0]
    return pl.pallas_call(
        functools.partial(_kv_head_body, q_lora=q_lora, kv_lora=kv_lora),
        grid=(T // tm, H),
        in_specs=[pl.BlockSpec((tm, CW), lambda i, h: (i, 0)),
                  pl.BlockSpec((1, kv_lora), lambda i, h: (0, 0)),
                  pl.BlockSpec((1, kv_lora, QK_NOPE), lambda i, h: (h, 0, 0)),
                  pl.BlockSpec((1, V_HEAD, kv_lora), lambda i, h: (h, 0, 0)),
                  pl.BlockSpec((1, QK_HEAD), lambda i, h: (0, 0)),
                  pl.BlockSpec((QK_HEAD, QK_HEAD), lambda i, h: (0, 0)),
                  pl.BlockSpec((tm, QK_HEAD), lambda i, h: (i, 0)),
                  pl.BlockSpec((tm, QK_HEAD), lambda i, h: (i, 0))],
        out_specs=[pl.BlockSpec((1, tm, QK_HEAD), lambda i, h: (h, i, 0)),
                   pl.BlockSpec((1, VT_ROWS, tm), lambda i, h: (h, 0, i))],
        out_shape=[jax.ShapeDtypeStruct((H, T, QK_HEAD), BF16),
                   jax.ShapeDtypeStruct((H, VT_ROWS, T), BF16)],
        scratch_shapes=[pltpu.VMEM((tm, kv_lora), BF16)],
        compiler_params=_cparams("parallel", "arbitrary"),
        name="mla_kv_heads",
    )(cqkv, g_kv.reshape(1, -1), w_uk_h, w_uvt_h, g_khead.reshape(1, -1), rot, cosf, sinf)


def _flash_body(q_ref, k_hbm, vt_hbm, o_prev_ref, o_ref, k_buf, vt_buf, acc_ref, s_buf, p_buf, sem,
                *, row0, n_chunks, last_valid, tk):
    del o_prev_ref
    h = pl.program_id(0)
    n_keys = k_buf.shape[0]

    @pl.when(pl.program_id(1) == 0)
    def _():
        ck = pltpu.make_async_copy(k_hbm.at[h, pl.ds(row0, n_keys)], k_buf, sem.at[0])
        cv = pltpu.make_async_copy(vt_hbm.at[h, :, pl.ds(row0, n_keys)], vt_buf, sem.at[1])
        ck.start()
        cv.start()
        ck.wait()
        cv.wait()

    q = q_ref[0]
    tq = q.shape[0]
    acc_ref[...] = jnp.zeros_like(acc_ref)

    def key_start(c):
        return c * tk if isinstance(c, int) else pl.multiple_of(c * tk, tk)

    def qk(c, slot, size=tk):
        s_buf[slot, :size] = lax.dot_general(k_buf[pl.ds(key_start(c), size), :], q, (((1,), (1,)), ((), ())),
                                             preferred_element_type=F32)

    def softmax(slot, m, size=tk, n_valid=None):
        s = s_buf[slot, :size]
        if n_valid is not None:
            s = jnp.where(lax.broadcasted_iota(jnp.int32, s.shape, 0) < n_valid, s, NEG_BIG)
        m_new = jnp.maximum(m, jnp.max(s, axis=0, keepdims=True))
        p_buf[slot, :size] = jnp.exp2(s - m_new).astype(BF16)
        return m_new, jnp.exp2(m - m_new)

    def pv(c, slot, alpha, size=tk):
        acc_ref[...] = alpha * acc_ref[...] + jnp.dot(vt_buf[:, pl.ds(key_start(c), size)], p_buf[slot, :size],
                                                      preferred_element_type=F32)

    def step(c, par, m, alpha, next_size=tk):
        pv(c, par, alpha)
        m, alpha = softmax(1 - par, m)
        qk(c + 2, par, next_size)
        return m, alpha

    def pair(i, carry):
        m, alpha = step(2 * i, 0, *carry)
        return step(2 * i + 1, 1, m, alpha)

    n = n_chunks
    last = -(-last_valid // LANE) * LANE
    qk(0, 0)
    qk(1, 1)
    m, alpha = softmax(0, jnp.full((1, tq), NEG_BIG, F32))
    m, alpha = lax.fori_loop(0, (n - 3) // 2, pair, (m, alpha))
    if (n - 3) % 2:
        m, alpha = step(n - 4, 0, m, alpha)
    m, alpha = step(n - 3, (n - 3) % 2, m, alpha, next_size=last)
    par = (n - 2) % 2
    pv(n - 2, par, alpha)
    m, alpha = softmax(1 - par, m, last, last_valid)
    pv(n - 1, 1 - par, alpha, last)
    acc = acc_ref[...]
    o = acc[:V_HEAD] / acc[V_HEAD:V_HEAD + 1]
    o_ref[...] = o.T.astype(o_ref.dtype)


def flash_attention(q, k, vt, o_prev, row0, n_rows, n_valid, tq=512, tk=512):
    H, T, _ = q.shape
    q0 = row0 // tq
    n_chunks = -(-n_valid // tk)
    last_valid = n_valid - (n_chunks - 1) * tk
    assert n_chunks >= 3 and n_chunks * tk <= n_rows
    return pl.pallas_call(
        functools.partial(_flash_body, row0=row0, n_chunks=n_chunks, last_valid=last_valid, tk=tk),
        grid=(H, n_rows // tq),
        in_specs=[pl.BlockSpec((1, tq, QK_HEAD), lambda h, i: (h, q0 + i, 0)),
                  pl.BlockSpec(memory_space=pl.ANY),
                  pl.BlockSpec(memory_space=pl.ANY),
                  pl.BlockSpec(memory_space=pl.ANY)],
        out_specs=pl.BlockSpec((tq, V_HEAD), lambda h, i: (q0 + i, h)),
        out_shape=jax.ShapeDtypeStruct((T, H * V_HEAD), BF16),
        scratch_shapes=[pltpu.VMEM((n_rows, QK_HEAD), BF16), pltpu.VMEM((VT_ROWS, n_rows), BF16),
                        pltpu.VMEM((VT_ROWS, tq), F32), pltpu.VMEM((2, tk, tq), F32), pltpu.VMEM((2, tk, tq), BF16),
                        pltpu.SemaphoreType.DMA((2,))],
        input_output_aliases={3: 0},
        compiler_params=_cparams("arbitrary", "arbitrary"),
        name="flash_attention",
    )(q, k, vt, o_prev)


META_E1, META_E2, META_G1, META_G2, META_R1, META_R2 = range(6)


def _router_body(h_ref, g_ref, wr_ref, xn_ref, meta_ref, cnt_ref, carry_ref, *, token_ranges):
    i = pl.program_id(0)

    @pl.when(i == 0)
    def _():
        carry_ref[...] = jnp.zeros_like(carry_ref)

    xn = _rms(h_ref[...], g_ref[...])
    xn_ref[...] = xn
    logits = jnp.dot(xn, wr_ref[...], precision=lax.Precision.HIGHEST, preferred_element_type=F32)
    tb = logits.shape[0]
    lane = lax.broadcasted_iota(jnp.int32, logits.shape, 1)
    lg = jnp.where(lane < N_EXPERTS, logits, NEG_BIG)
    m1 = jnp.max(lg, axis=-1, keepdims=True)
    i1 = jnp.min(jnp.where(lg == m1, lane, LANE), axis=-1, keepdims=True)
    lg2 = jnp.where(lane == i1, NEG_BIG, lg)
    m2 = jnp.max(lg2, axis=-1, keepdims=True)
    i2 = jnp.min(jnp.where(lg2 == m2, lane, LANE), axis=-1, keepdims=True)
    e = jnp.exp(m2 - m1)
    row = lax.broadcasted_iota(jnp.int32, (tb, 1), 0) + i * tb
    valid = functools.reduce(jnp.logical_or, [(row >= a) & (row < b) for a, b in token_ranges])
    g1 = jnp.where(valid, 1.0 / (1.0 + e), 0.0)
    g2 = jnp.where(valid, e / (1.0 + e), 0.0)
    oh = jnp.where(valid & ((lane == i1) | (lane == i2)), 1.0, 0.0)
    r = lax.broadcasted_iota(jnp.int32, (tb, tb), 0)
    c = lax.broadcasted_iota(jnp.int32, (tb, tb), 1)
    tri = jnp.where(c < r, 1.0, 0.0).astype(BF16)
    pre = jnp.dot(tri, oh.astype(BF16), preferred_element_type=F32) + carry_ref[0:1, :]
    r1 = jnp.sum(jnp.where(lane == i1, pre, 0.0), axis=-1, keepdims=True)
    r2 = jnp.sum(jnp.where(lane == i2, pre, 0.0), axis=-1, keepdims=True)
    carry_ref[...] += jnp.broadcast_to(jnp.sum(oh, axis=0, keepdims=True), carry_ref.shape)
    meta = jnp.zeros(logits.shape, F32)
    for idx, val in ((META_E1, i1.astype(F32)), (META_E2, i2.astype(F32)), (META_G1, g1), (META_G2, g2),
                     (META_R1, r1), (META_R2, r2)):
        meta = jnp.where(lane == idx, val, meta)
    meta_ref[...] = meta
    cnt_ref[...] = carry_ref[...]


def moe_router(h, g, w_router_pad, token_ranges, tb=256):
    T, D = h.shape
    return pl.pallas_call(
        functools.partial(_router_body, token_ranges=token_ranges),
        grid=(T // tb,),
        in_specs=[pl.BlockSpec((tb, D), lambda i: (i, 0)),
                  pl.BlockSpec((1, D), lambda i: (0, 0)),
                  pl.BlockSpec((D, LANE), lambda i: (0, 0))],
        out_specs=[pl.BlockSpec((tb, D), lambda i: (i, 0)),
                   pl.BlockSpec((tb, LANE), lambda i: (i, 0)),
                   pl.BlockSpec((SUBLANE, LANE), lambda i: (0, 0))],
        out_shape=[jax.ShapeDtypeStruct((T, D), F32),
                   jax.ShapeDtypeStruct((T, LANE), F32),
                   jax.ShapeDtypeStruct((SUBLANE, LANE), F32)],
        scratch_shapes=[pltpu.VMEM((SUBLANE, LANE), F32)],
        compiler_params=_cparams("arbitrary"),
        name="moe_router",
    )(h, g.reshape(1, D), w_router_pad)


def _expert_up_body(te_ref, nu_ref, src_ref, xn_hbm, wg_ref, wu_ref, o_ref, xf_ref, xb_ref, sem):
    active = pl.program_id(0) < nu_ref[0]
    tm = xf_ref.shape[0]

    @pl.when(active & (pl.program_id(1) == 0))
    def _():
        def row_copy(r):
            return pltpu.make_async_copy(xn_hbm.at[pl.ds(src_ref[0, 0, r], 1)], xf_ref.at[pl.ds(r, 1)], sem)

        def start(r, c):
            row_copy(r).start()
            return c

        def wait(r, c):
            row_copy(r).wait()
            return c

        lax.fori_loop(0, tm, start, 0, unroll=DMA_UNROLL)
        lax.fori_loop(0, tm, wait, 0, unroll=DMA_UNROLL)
        xb_ref[...] = xf_ref[...].astype(BF16)

    @pl.when(active)
    def _():
        x = xb_ref[...]
        a = jnp.dot(x, wg_ref[0], preferred_element_type=F32)
        b = jnp.dot(x, wu_ref[0], preferred_element_type=F32)
        o_ref[...] = _silu_mul(a, b).astype(o_ref.dtype)

    @pl.when(jnp.logical_not(active))
    def _():
        o_ref[...] = jnp.zeros_like(o_ref)


def expert_up(xn, src, wg, wu, tile_expert, n_used, tm, tf=1024):
    D = xn.shape[1]
    n_tiles = src.shape[0] // tm
    F = wg.shape[2]
    tf = _tile(F, tf)
    grid_spec = pltpu.PrefetchScalarGridSpec(
        num_scalar_prefetch=2,
        grid=(n_tiles, F // tf),
        in_specs=[pl.BlockSpec((1, 1, tm), lambda i, j, te, nu: (i, 0, 0), memory_space=pltpu.SMEM),
                  pl.BlockSpec(memory_space=pl.ANY),
                  pl.BlockSpec((1, D, tf), lambda i, j, te, nu: (te[i], 0, j)),
                  pl.BlockSpec((1, D, tf), lambda i, j, te, nu: (te[i], 0, j))],
        out_specs=pl.BlockSpec((tm, tf), lambda i, j, te, nu: (i, j)),
        scratch_shapes=[pltpu.VMEM((tm, D), F32), pltpu.VMEM((tm, D), BF16), pltpu.SemaphoreType.DMA(())],
    )
    return pl.pallas_call(
        _expert_up_body,
        grid_spec=grid_spec,
        out_shape=jax.ShapeDtypeStruct((n_tiles * tm, F), BF16),
        compiler_params=_cparams("arbitrary", "arbitrary"),
        name="moe_expert_up",
    )(tile_expert, n_used, src.reshape(n_tiles, 1, tm), xn, wg, wu)


def _expert_down_body(te_ref, nu_ref, a_ref, w_ref, o_ref):
    @pl.when(pl.program_id(0) < nu_ref[0])
    def _():
        o_ref[...] = jnp.dot(a_ref[...], w_ref[0], preferred_element_type=F32)

    @pl.when(pl.program_id(0) >= nu_ref[0])
    def _():
        o_ref[...] = jnp.zeros_like(o_ref)


def expert_down(act, wd, tile_expert, n_used, tm, tn=512):
    S, F = act.shape
    D = wd.shape[2]
    tn = _tile(D, tn)
    grid_spec = pltpu.PrefetchScalarGridSpec(
        num_scalar_prefetch=2,
        grid=(S // tm, D // tn),
        in_specs=[pl.BlockSpec((tm, F), lambda i, j, te, nu: (i, 0)),
                  pl.BlockSpec((1, F, tn), lambda i, j, te, nu: (te[i], 0, j))],
        out_specs=pl.BlockSpec((tm, tn), lambda i, j, te, nu: (i, j)),
    )
    return pl.pallas_call(
        _expert_down_body,
        grid_spec=grid_spec,
        out_shape=jax.ShapeDtypeStruct((S, D), F32),
        compiler_params=_cparams("parallel", "parallel"),
        name="moe_expert_down",
    )(tile_expert, n_used, act, wd)


def _combine_body(pos_ref, h_ref, meta_ref, ys_hbm, o_ref, buf, sem, *, tb):
    def row_copy(r, k):
        return pltpu.make_async_copy(ys_hbm.at[pl.ds(pos_ref[0, 0, 2 * r + k], 1)],
                                     buf.at[k, pl.ds(r, 1)], sem)

    def start(r, c):
        row_copy(r, 0).start()
        row_copy(r, 1).start()
        return c

    def wait(r, c):
        row_copy(r, 0).wait()
        row_copy(r, 1).wait()
        return c

    lax.fori_loop(0, tb, start, 0, unroll=DMA_UNROLL)
    lax.fori_loop(0, tb, wait, 0, unroll=DMA_UNROLL)
    g1 = meta_ref[:, META_G1:META_G1 + 1]
    g2 = meta_ref[:, META_G2:META_G2 + 1]
    o_ref[...] = h_ref[...] + g1 * buf[0] + g2 * buf[1]


def moe_combine(h, meta, pos, ys, tb=256):
    T, D = h.shape
    pos3 = pos.reshape(T // tb, 1, 2 * tb)
    return pl.pallas_call(
        functools.partial(_combine_body, tb=tb),
        grid=(T // tb,),
        in_specs=[pl.BlockSpec((1, 1, 2 * tb), lambda i: (i, 0, 0), memory_space=pltpu.SMEM),
                  pl.BlockSpec((tb, D), lambda i: (i, 0)),
                  pl.BlockSpec((tb, LANE), lambda i: (i, 0)),
                  pl.BlockSpec(memory_space=pl.ANY)],
        out_specs=pl.BlockSpec((tb, D), lambda i: (i, 0)),
        out_shape=jax.ShapeDtypeStruct((T, D), F32),
        scratch_shapes=[pltpu.VMEM((2, tb, D), F32), pltpu.SemaphoreType.DMA(())],
        compiler_params=_cparams("arbitrary"),
        name="moe_combine",
    )(pos3, h, meta, ys)


def moe_layer(h, g, w_router, wg, wu, wd, token_ranges, tm=512):
    T, D = h.shape
    wr = jnp.pad(w_router, ((0, 0), (0, LANE - N_EXPERTS)))
    xn, meta, cnt = moe_router(h, g, wr, token_ranges)
    counts = cnt[0, :N_EXPERTS].astype(jnp.int32)
    tiles = (counts + tm - 1) // tm
    tile_end = jnp.cumsum(tiles)
    starts = (tile_end - tiles) * tm
    n_tok = sum(b - a for a, b in token_ranges)
    n_tiles = (2 * n_tok + N_EXPERTS * (tm - 1)) // tm
    e_idx = meta[:, META_E1:META_E2 + 1].astype(jnp.int32)
    rank = meta[:, META_R1:META_R2 + 1].astype(jnp.int32)
    row = jnp.arange(T, dtype=jnp.int32)[:, None]
    valid = functools.reduce(jnp.logical_or, [(row >= a) & (row < b) for a, b in token_ranges])
    pos = starts[e_idx] + rank
    tile_expert = jnp.minimum(jnp.sum(jnp.arange(n_tiles)[:, None] >= tile_end[None, :], axis=1),
                              N_EXPERTS - 1).astype(jnp.int32)
    n_used = tile_end[-1:].astype(jnp.int32)
    tok = jnp.repeat(row[:, 0], 2)
    src = jnp.zeros((n_tiles * tm,), jnp.int32).at[jnp.where(valid, pos, n_tiles * tm).reshape(-1)].set(
        tok, mode="drop")
    act = expert_up(xn, src, wg, wu, tile_expert, n_used, tm)
    ys = expert_down(act, wd, tile_expert, n_used, tm)
    return moe_combine(h, meta, jnp.where(valid, pos, 0), ys)


def _padded_len(L):
    r = -(-L // FFT_C)
    r_align = TOK_TILE // FFT_C
    r = -(-r // r_align) * r_align
    return r * FFT_C


def _rope_tables(lens, pads):
    half = QK_ROPE // 2
    inv = 1.0 / (ROPE_BASE ** (jnp.arange(0, QK_ROPE, 2, dtype=F32) / QK_ROPE))
    cs, ss = [], []
    for L, P in zip(lens, pads):
        ang = jnp.arange(P, dtype=F32)[:, None] * inv[None, :]
        c, s = jnp.cos(ang), jnp.sin(ang)
        cs.append(jnp.concatenate([jnp.ones((P, QK_NOPE), F32), c, c], axis=-1))
        ss.append(jnp.concatenate([jnp.zeros((P, QK_NOPE), F32), s, s], axis=-1))
    rot = np.zeros((QK_HEAD, QK_HEAD), np.float32)
    for c in range(QK_NOPE, QK_NOPE + half):
        rot[c + half, c] = -1.0
        rot[c, c + half] = 1.0
    return jnp.concatenate(cs), jnp.concatenate(ss), jnp.asarray(rot, BF16)


def kernel(x_prompt, x_sample, meta_tokens, norm_mix, norm_ffn, hy_w_in, hy_conv_w, hy_conv_b, hy_filt_w1, hy_filt_b1, hy_filt_w2, hy_filt_b2, hy_filt_w3, hy_filt_freq, hy_filt_log_decay, hy_skip, hy_w_out, ffn_w_gate, ffn_w_up, ffn_w_down, mla_w_dq, mla_g_q, mla_w_uq, mla_w_dkv, mla_g_kv, mla_w_ukv, mla_g_qhead, mla_g_khead, mla_w_o, moe_w_router, moe_w_gate, moe_w_up, moe_w_down):
    D = x_prompt.shape[-1]
    n_meta = meta_tokens.shape[0]
    assert x_prompt.shape[0] == 1 and x_sample.shape[0] == 1
    groups = (x_prompt[0], x_sample[0])
    lens = [n_meta + g.shape[0] for g in groups]
    pads = [_padded_len(L) for L in lens]
    row0s = [0, pads[0]]
    T = sum(pads)
    assert T % TOK_TILE == 0 and all(p % TOK_TILE == 0 for p in pads)

    parts = []
    for g, L, P in zip(groups, lens, pads):
        parts += [meta_tokens.astype(F32), g, jnp.zeros((P - L, D), F32)]
    h = jnp.concatenate(parts, axis=0)

    bf = lambda w: w.astype(BF16)

    u = norm_mm(h, norm_mix[0], [bf(hy_w_in)], lambda y: y, BF16, 2048, "hyena_in_proj")
    n_orders = hy_skip.shape[0]
    fh = hy_filt_w2.shape[0]
    filt = (jnp.pad(hy_filt_w1, ((0, LANE - POS_EMB_DIM), (0, 0))), hy_filt_b1, hy_filt_w2, hy_filt_b2,
            bf(hy_filt_w3.reshape(fh, n_orders * 2 * D)), hy_filt_freq, hy_filt_log_decay.reshape(-1))
    zs = []
    for L, P, r0 in zip(lens, pads, row0s):
        uc = short_conv(u, hy_conv_w, hy_conv_b, r0, P, L)
        zs.append(hyena_group(uc, L, D, filt, hy_skip))
    z = jnp.concatenate(zs, axis=0)
    h = mm_res(z, bf(hy_w_out), h, 2048, "hyena_out_proj")

    a = norm_mm(h, norm_ffn[0], [bf(ffn_w_gate), bf(ffn_w_up)], _silu_mul, BF16, 1024, "ffn_swiglu")
    h = mm_res(a, bf(ffn_w_down), h, 512, "ffn_down")

    q_lora = mla_w_dq.shape[1]
    kv_lora = mla_w_ukv.shape[0]
    cw = q_lora + mla_w_dkv.shape[1]
    cw_pad = -(-cw // LANE) * LANE
    w_down = jnp.pad(jnp.concatenate([mla_w_dq, mla_w_dkv], axis=1), ((0, 0), (0, cw_pad - cw)))
    cqkv = norm_mm(h, norm_mix[1], [bf(w_down)], lambda y: y, F32, cw_pad, "mla_down_proj")
    cosf, sinf, rot = _rope_tables(lens, pads)
    w_uq_h = bf(mla_w_uq.reshape(q_lora, MLA_HEADS, QK_HEAD).transpose(1, 0, 2))
    w_ukv_h = mla_w_ukv.reshape(kv_lora, MLA_HEADS, QK_NOPE + V_HEAD).transpose(1, 0, 2)
    w_uk_h = bf(w_ukv_h[:, :, :QK_NOPE])
    w_uvt_h = bf(w_ukv_h[:, :, QK_NOPE:].transpose(0, 2, 1))
    q = mla_q_heads(cqkv, mla_g_q, w_uq_h, mla_g_qhead, rot, cosf, sinf, q_lora)
    k, vt = mla_kv_heads(cqkv, mla_g_kv, w_uk_h, w_uvt_h, mla_g_khead, rot, cosf, sinf, q_lora, kv_lora)
    o = jnp.zeros((T, MLA_HEADS * V_HEAD), BF16)
    for L, P, r0 in zip(lens, pads, row0s):
        o = flash_attention(q, k, vt, o, r0, P, L)
    h = mm_res(o, bf(mla_w_o), h, 2048, "mla_out_proj")

    token_ranges = tuple((r0, r0 + L) for r0, L in zip(row0s, lens))
    h = moe_layer(h, norm_ffn[1], moe_w_router, bf(moe_w_gate), bf(moe_w_up), bf(moe_w_down), token_ranges)

    outs = []
    for g, r0 in zip(groups, row0s):
        outs.append(h[r0 + n_meta:r0 + n_meta + g.shape[0]][None])
    return tuple(outs)
```
